```python
import jax, jax.numpy as jnp
from jax import lax
import numpy as np

D_MODEL = 2048
BATCH = 2
SEQ = 4096
DEPTH = 2
DEC_BATCH = 128
DEC_SEQ = 1
PAST_LEN = 8192
PAGE_SIZE = 128

N_META = 16
N_MIXERS = 2
N_CONV_LAYERS = (DEPTH + 1) // 2
N_MLA_LAYERS = DEPTH // 2
D_CONV = D_MODEL
CONV_WIDTH = 3
N_HEADS = 16
QK_NOPE = 128
QK_ROPE = 64
V_HEAD = 128
Q_LORA = 512
KV_LORA = 512
ROPE_THETA = 10000.0
D_FF = 5632
Q_BLOCK = 128
LN_EPS = 1e-5
RMS_EPS = 1e-6
ATTN_SCALE = (QK_NOPE + QK_ROPE) ** -0.5
DEEPNORM_ALPHA = (2 * DEPTH) ** 0.25
DEEPNORM_BETA = (8 * DEPTH) ** -0.25

kernel_name = "hybrid_shortconv_mla_macaron_deepnorm_step"


def layer_norm(x, g, b):
    xf = x.astype(jnp.float32)
    mu = jnp.mean(xf, -1, keepdims=True)
    var = jnp.mean(jnp.square(xf - mu), -1, keepdims=True)
    y = (xf - mu) * lax.rsqrt(var + LN_EPS) * g.astype(jnp.float32) + b.astype(jnp.float32)
    return y.astype(x.dtype)


def rms_norm(x, g):
    xf = x.astype(jnp.float32)
    y = xf * lax.rsqrt(jnp.mean(jnp.square(xf), -1, keepdims=True) + RMS_EPS) * g.astype(jnp.float32)
    return y.astype(x.dtype)


def rope(x, pos):
    inv = ROPE_THETA ** (-jnp.arange(0, QK_ROPE, 2, dtype=jnp.float32) / QK_ROPE)
    ang = pos.astype(jnp.float32)[:, None] * inv[None, :]
    if x.ndim == 4:
        ang = ang[:, None, :]
    cos, sin = jnp.cos(ang), jnp.sin(ang)
    x1, x2 = jnp.split(x.astype(jnp.float32), 2, axis=-1)
    return jnp.concatenate([x1 * cos - x2 * sin, x1 * sin + x2 * cos], -1).astype(x.dtype)


def swiglu(x, wg, wu, wd):
    return (jax.nn.silu(x @ wg) * (x @ wu)) @ wd


def macaron_half(x, wg, wu, wd, g, b):
    return layer_norm(DEEPNORM_ALPHA * x + 0.5 * swiglu(x, wg, wu, wd), g, b)


def short_conv_mixer(x, u_prev, w_in, conv_w, w_out):
    gate_b, gate_c, h = jnp.split(x @ w_in, 3, axis=-1)
    u = gate_c * h
    u_ext = jnp.concatenate([u_prev.astype(u.dtype), u], axis=1)
    conv = lax.conv_general_dilated(u_ext, conv_w[:, None, :].astype(u.dtype), window_strides=(1,), padding='VALID',
                                    dimension_numbers=('NWC', 'WIO', 'NWC'), feature_group_count=D_CONV)
    y = (gate_b * conv) @ w_out
    return y, u_ext[:, -(CONV_WIDTH - 1):]


def mla_project(x, pos, w_q_a, g_q_a, w_q_b, w_kv_a, g_kv_a):
    c_q = rms_norm(x @ w_q_a, g_q_a)
    q = jnp.einsum('btc,chd->bthd', c_q, w_q_b)
    q_nope, q_rope = q[..., :QK_NOPE], rope(q[..., QK_NOPE:], pos)
    kv = x @ w_kv_a
    c_kv = rms_norm(kv[..., :KV_LORA], g_kv_a)
    k_rope = rope(kv[..., KV_LORA:], pos)
    return q_nope, q_rope, c_kv, k_rope


def mla_prompt_attention(q_nope, q_rope, c_kv, k_rope, w_uk, w_uv):
    nb, t = q_nope.shape[:2]
    k_nope = jnp.einsum('btc,chd->bthd', c_kv, w_uk)
    v = jnp.einsum('btc,chd->bthd', c_kv, w_uv)
    k_pos = jnp.arange(t)

    def block(args):
        qn, qr, q_pos = args
        s = (jnp.einsum('bqhd,bkhd->bhqk', qn, k_nope) + jnp.einsum('bqhr,bkr->bhqk', qr, k_rope)).astype(jnp.float32) * ATTN_SCALE
        s = jnp.where(k_pos[None, :] <= q_pos[:, None], s, -jnp.inf)
        p = jax.nn.softmax(s, axis=-1).astype(v.dtype)
        return jnp.einsum('bhqk,bkhd->bqhd', p, v)

    o_meta = block((q_nope[:, :N_META], q_rope[:, :N_META], k_pos[:N_META]))
    n_blk = (t - N_META) // Q_BLOCK

    def to_blocks(a):
        return jnp.moveaxis(a[:, N_META:].reshape(nb, n_blk, Q_BLOCK, *a.shape[2:]), 1, 0)

    o_rest = lax.map(block, (to_blocks(q_nope), to_blocks(q_rope), k_pos[N_META:].reshape(n_blk, Q_BLOCK)))
    o_rest = jnp.moveaxis(o_rest, 0, 1).reshape(nb, t - N_META, N_HEADS, V_HEAD)
    return jnp.concatenate([o_meta, o_rest], axis=1)


def mla_sample_attention(q_nope, q_rope, c_kv_new, k_rope_new, ckv_pool, kr_pool, page_table, w_uk, w_uv):
    nb, s_new = q_nope.shape[:2]
    past = page_table.shape[1] * ckv_pool.shape[1]
    ckv = jnp.concatenate([ckv_pool[page_table].reshape(nb, past, KV_LORA).astype(c_kv_new.dtype), c_kv_new], axis=1)
    kr = jnp.concatenate([kr_pool[page_table].reshape(nb, past, QK_ROPE).astype(k_rope_new.dtype), k_rope_new], axis=1)
    q_lat = jnp.einsum('bqhd,chd->bqhc', q_nope, w_uk)
    s = (jnp.einsum('bqhc,bkc->bhqk', q_lat, ckv) + jnp.einsum('bqhr,bkr->bhqk', q_rope, kr)).astype(jnp.float32) * ATTN_SCALE
    k_idx = jnp.arange(past + s_new)
    q_idx = past + jnp.arange(s_new)
    s = jnp.where(k_idx[None, :] <= q_idx[:, None], s, -jnp.inf)
    p = jax.nn.softmax(s, axis=-1).astype(ckv.dtype)
    o_lat = jnp.einsum('bhqk,bkc->bqhc', p, ckv)
    return jnp.einsum('bqhc,chd->bqhd', o_lat, w_uv)


def setup_inputs(seed: int = 0) -> dict:
    key = jax.random.key(seed)
    ks = iter(jax.random.split(key, 40))
    f32 = jnp.float32

    def nrm(shape, scale=1.0):
        return jax.random.normal(next(ks), shape, f32) * scale

    n_pages = PAST_LEN // PAGE_SIZE
    n_used = DEC_BATCH * n_pages
    n_pool = n_used + max(n_used // 4, 1)
    page_table = jax.random.permutation(next(ks), n_pool)[:n_used].reshape(DEC_BATCH, n_pages).astype(jnp.int32)

    return {
        "x_prompt": nrm((BATCH, SEQ, D_MODEL)),
        "x_sample": nrm((DEC_BATCH, DEC_SEQ, D_MODEL)),
        "state_conv": nrm((N_CONV_LAYERS, DEC_BATCH, CONV_WIDTH - 1, D_CONV)),
        "cache_kv_latent": nrm((N_MLA_LAYERS, n_pool, PAGE_SIZE, KV_LORA)),
        "cache_k_rope": nrm((N_MLA_LAYERS, n_pool, PAGE_SIZE, QK_ROPE)),
        "page_table": page_table,
        "meta_tokens": nrm((N_META, D_MODEL)),
        "ln_g": 1.0 + nrm((DEPTH, 3, D_MODEL), 0.02),
        "ln_b": nrm((DEPTH, 3, D_MODEL), 0.02),
        "w_ffn_gate": nrm((DEPTH, 2, D_MODEL, D_FF), D_MODEL ** -0.5),
        "w_ffn_up": nrm((DEPTH, 2, D_MODEL, D_FF), D_MODEL ** -0.5),
        "w_ffn_down": nrm((DEPTH, 2, D_FF, D_MODEL), DEEPNORM_BETA * D_FF ** -0.5),
        "w_conv_in": nrm((N_CONV_LAYERS, D_MODEL, 3 * D_CONV), D_MODEL ** -0.5),
        "w_conv": nrm((N_CONV_LAYERS, CONV_WIDTH, D_CONV), CONV_WIDTH ** -0.5),
        "w_conv_out": nrm((N_CONV_LAYERS, D_CONV, D_MODEL), DEEPNORM_BETA * D_CONV ** -0.5),
        "w_q_a": nrm((N_MLA_LAYERS, D_MODEL, Q_LORA), D_MODEL ** -0.5),
        "g_q_a": 1.0 + nrm((N_MLA_LAYERS, Q_LORA), 0.02),
        "w_q_b": nrm((N_MLA_LAYERS, Q_LORA, N_HEADS, QK_NOPE + QK_ROPE), Q_LORA ** -0.5),
        "w_kv_a": nrm((N_MLA_LAYERS, D_MODEL, KV_LORA + QK_ROPE), D_MODEL ** -0.5),
        "g_kv_a": 1.0 + nrm((N_MLA_LAYERS, KV_LORA), 0.02),
        "w_uk": nrm((N_MLA_LAYERS, KV_LORA, N_HEADS, QK_NOPE), KV_LORA ** -0.5),
        "w_uv": nrm((N_MLA_LAYERS, KV_LORA, N_HEADS, V_HEAD), KV_LORA ** -0.5),
        "w_o": nrm((N_MLA_LAYERS, N_HEADS * V_HEAD, D_MODEL), DEEPNORM_BETA * (N_HEADS * V_HEAD) ** -0.5),
    }


def reference(x_prompt, x_sample, state_conv, cache_kv_latent, cache_k_rope, page_table, meta_tokens,
              ln_g, ln_b, w_ffn_gate, w_ffn_up, w_ffn_down, w_conv_in, w_conv, w_conv_out,
              w_q_a, g_q_a, w_q_b, w_kv_a, g_kv_a, w_uk, w_uv, w_o):
    nb = x_prompt.shape[0]
    meta = jnp.broadcast_to(meta_tokens.astype(x_prompt.dtype)[None], (nb, N_META, D_MODEL))
    xp = jnp.concatenate([meta, x_prompt], axis=1)
    xs = x_sample
    t_p = xp.shape[1]
    past = page_table.shape[1] * cache_kv_latent.shape[2]
    pos_p = jnp.arange(t_p)
    pos_s = past + jnp.arange(xs.shape[1])

    conv_p, conv_s, ckv_p, ckv_s, kr_p, kr_s = [], [], [], [], [], []
    for i in range(DEPTH):
        xp = macaron_half(xp, w_ffn_gate[i, 0], w_ffn_up[i, 0], w_ffn_down[i, 0], ln_g[i, 0], ln_b[i, 0])
        xs = macaron_half(xs, w_ffn_gate[i, 0], w_ffn_up[i, 0], w_ffn_down[i, 0], ln_g[i, 0], ln_b[i, 0])
        j = i // N_MIXERS
        if i % N_MIXERS == 0:
            zeros = jnp.zeros((nb, CONV_WIDTH - 1, D_CONV), xp.dtype)
            mp, sp = short_conv_mixer(xp, zeros, w_conv_in[j], w_conv[j], w_conv_out[j])
            ms, ss = short_conv_mixer(xs, state_conv[j], w_conv_in[j], w_conv[j], w_conv_out[j])
            conv_p.append(sp)
            conv_s.append(ss)
        else:
            qn, qr, ckv, kr = mla_project(xp, pos_p, w_q_a[j], g_q_a[j], w_q_b[j], w_kv_a[j], g_kv_a[j])
            o = mla_prompt_attention(qn, qr, ckv, kr, w_uk[j], w_uv[j])
            mp = o.reshape(nb, t_p, N_HEADS * V_HEAD) @ w_o[j]
            ckv_p.append(ckv)
            kr_p.append(kr)
            qn, qr, ckv, kr = mla_project(xs, pos_s, w_q_a[j], g_q_a[j], w_q_b[j], w_kv_a[j], g_kv_a[j])
            o = mla_sample_attention(qn, qr, ckv, kr, cache_kv_latent[j], cache_k_rope[j], page_table, w_uk[j], w_uv[j])
            ms = o.reshape(xs.shape[0], xs.shape[1], N_HEADS * V_HEAD) @ w_o[j]
            ckv_s.append(ckv)
            kr_s.append(kr)
        xp = layer_norm(DEEPNORM_ALPHA * xp + mp, ln_g[i, 1], ln_b[i, 1])
        xs = layer_norm(DEEPNORM_ALPHA * xs + ms, ln_g[i, 1], ln_b[i, 1])
        xp = macaron_half(xp, w_ffn_gate[i, 1], w_ffn_up[i, 1], w_ffn_down[i, 1], ln_g[i, 2], ln_b[i, 2])
        xs = macaron_half(xs, w_ffn_gate[i, 1], w_ffn_up[i, 1], w_ffn_down[i, 1], ln_g[i, 2], ln_b[i, 2])

    y_prompt = xp[:, N_META:]
    y_sample = xs
    return (y_prompt, y_sample, jnp.stack(conv_p), jnp.stack(conv_s), jnp.stack(ckv_p), jnp.stack(ckv_s),
            jnp.stack(kr_p), jnp.stack(kr_s))
```

```python
import functools

import jax
import jax.numpy as jnp
from jax import lax
from jax.experimental import pallas as pl
from jax.experimental.pallas import tpu as pltpu

LN_EPS = 1e-5
RMS_EPS = 1e-6
ROPE_THETA = 10000.0
N_MIXERS = 2

LANES = 128
SUBLANES = 8
VMEM_LIMIT_BYTES = 56 * 1024 * 1024

BF16 = jnp.bfloat16
F32 = jnp.float32


def _tile(n, pref):
    if n <= pref:
        return n
    for t in range(pref, 15, -1):
        if n % t == 0 and t % 16 == 0:
            return t
    return n


def _params(*sem):
    return pltpu.CompilerParams(dimension_semantics=sem, vmem_limit_bytes=VMEM_LIMIT_BYTES)


def _resident(shape, index_map):
    return pl.BlockSpec(shape, index_map, pipeline_mode=pl.Buffered(1))


def _layer_norm(y, g, b):
    mu = jnp.mean(y, axis=-1, keepdims=True)
    yc = y - mu
    var = jnp.mean(yc * yc, axis=-1, keepdims=True)
    return yc * lax.rsqrt(var + LN_EPS) * g + b


def _rms_norm(y, g):
    return y * lax.rsqrt(jnp.mean(y * y, axis=-1, keepdims=True) + RMS_EPS) * g


def _dot(a, b):
    return jnp.dot(a, b, preferred_element_type=F32)


def _dot_nt(a, b):
    return lax.dot_general(a, b, (((1,), (1,)), ((), ())), preferred_element_type=F32)


def _ffn_kernel(x_ref, wg_ref, wu_ref, wd_ref, g_ref, b_ref, o_ref, xb_ref, acc_ref, *, alpha):
    f = pl.program_id(1)

    @pl.when(f == 0)
    def _():
        xb_ref[...] = x_ref[...].astype(BF16)
        acc_ref[...] = jnp.zeros_like(acc_ref)

    xb = xb_ref[...]
    gate = _dot(xb, wg_ref[...])
    up = _dot(xb, wu_ref[...])
    h = gate / (1.0 + jnp.exp(-gate)) * up
    acc_ref[...] += _dot(h.astype(BF16), wd_ref[...])

    @pl.when(f == pl.num_programs(1) - 1)
    def _():
        y = alpha * x_ref[...] + 0.5 * acc_ref[...]
        o_ref[...] = _layer_norm(y, g_ref[...], b_ref[...])


def _ffn(x, wg, wu, wd, g, b, layer, half, ln_idx, alpha, tm_pref=512, tf_pref=512):
    m, d = x.shape
    ff = wg.shape[-1]
    tm, tf = _tile(m, tm_pref), _tile(ff, tf_pref)
    return pl.pallas_call(
        functools.partial(_ffn_kernel, alpha=alpha),
        grid=(m // tm, ff // tf),
        in_specs=[
            pl.BlockSpec((tm, d), lambda i, f: (i, 0)),
            pl.BlockSpec((None, None, d, tf), lambda i, f: (layer, half, 0, f)),
            pl.BlockSpec((None, None, d, tf), lambda i, f: (layer, half, 0, f)),
            pl.BlockSpec((None, None, tf, d), lambda i, f: (layer, half, f, 0)),
            pl.BlockSpec((None, None, 1, d), lambda i, f: (layer, ln_idx, 0, 0)),
            pl.BlockSpec((None, None, 1, d), lambda i, f: (layer, ln_idx, 0, 0)),
        ],
        out_specs=pl.BlockSpec((tm, d), lambda i, f: (i, 0)),
        out_shape=jax.ShapeDtypeStruct((m, d), F32),
        scratch_shapes=[pltpu.VMEM((tm, d), BF16), pltpu.VMEM((tm, d), F32)],
        compiler_params=_params("parallel", "arbitrary"),
        name="ffn_macaron",
    )(x, wg, wu, wd, g, b)


def _conv_taps(u, u_m1, u_m2, wconv_ref):
    w = wconv_ref[...]
    return w[0:1] * u_m2 + w[1:2] * u_m1 + w[2:3] * u


def _conv_seq_kernel(x_ref, prev_ref, wb_ref, wc_ref, wh_ref, wconv_ref, wout_ref, g_ref, b_ref,
                     o_ref, ulast_ref, xb_ref, acc_ref, carry_ref, *, alpha, tiles_per_seq):
    i, c = pl.program_id(0), pl.program_id(1)
    tm = x_ref.shape[0]

    @pl.when(c == 0)
    def _():
        xb_ref[...] = x_ref[...].astype(BF16)
        acc_ref[...] = jnp.zeros_like(acc_ref)

    @pl.when(i == 0)
    def _():
        carry_ref[c] = jnp.zeros(carry_ref.shape[1:], F32)

    xb = xb_ref[...]
    u = _dot(xb, wc_ref[...]) * _dot(xb, wh_ref[...])
    seq_start = (i % tiles_per_seq) == 0
    tail = jnp.where(seq_start, prev_ref[...], carry_ref[c])
    row = lax.broadcasted_iota(jnp.int32, u.shape, 0)
    u_m1 = jnp.where(row == 0, tail[7:8], pltpu.roll(u, 1, 0))
    u_m2 = jnp.where(row == 0, tail[6:7], jnp.where(row == 1, tail[7:8], pltpu.roll(u, 2, 0)))
    conv = _conv_taps(u, u_m1, u_m2, wconv_ref)
    last = u[tm - SUBLANES:, :]
    carry_ref[c] = last
    ulast_ref[...] = last
    z = _dot(xb, wb_ref[...]) * conv
    acc_ref[...] += _dot(z.astype(BF16), wout_ref[...])

    @pl.when(c == pl.num_programs(1) - 1)
    def _():
        o_ref[...] = _layer_norm(alpha * x_ref[...] + acc_ref[...], g_ref[...], b_ref[...])


def _conv_rows_kernel(x_ref, p1_ref, p2_ref, f1_ref, f2_ref, wb_ref, wc_ref, wh_ref, wconv_ref, wout_ref,
                      g_ref, b_ref, o_ref, u_ref, xb_ref, acc_ref, *, alpha):
    c = pl.program_id(0)

    @pl.when(c == 0)
    def _():
        xb_ref[...] = x_ref[...].astype(BF16)
        acc_ref[...] = jnp.zeros_like(acc_ref)

    xb = xb_ref[...]
    u = _dot(xb, wc_ref[...]) * _dot(xb, wh_ref[...])
    u_m1 = jnp.where(f1_ref[...] > 0.5, p1_ref[...], pltpu.roll(u, 1, 0))
    u_m2 = jnp.where(f2_ref[...] > 0.5, p2_ref[...], pltpu.roll(u, 2, 0))
    conv = _conv_taps(u, u_m1, u_m2, wconv_ref)
    u_ref[...] = u
    z = _dot(xb, wb_ref[...]) * conv
    acc_ref[...] += _dot(z.astype(BF16), wout_ref[...])

    @pl.when(c == pl.num_programs(0) - 1)
    def _():
        o_ref[...] = _layer_norm(alpha * x_ref[...] + acc_ref[...], g_ref[...], b_ref[...])


def _conv_mixer_seq(x, prev, w_in, w_conv, w_out, g, b, j, layer, alpha, seq_len, tm_pref=512, tn_pref=512):
    m, d = x.shape
    dc = w_out.shape[1]
    tm, tn = _tile(seq_len, tm_pref), _tile(dc, tn_pref)
    tiles_per_seq = seq_len // tm
    n_c = dc // tn
    w_spec = lambda part: pl.BlockSpec((None, d, tn), lambda i, c: (j, 0, part * n_c + c))
    return pl.pallas_call(
        functools.partial(_conv_seq_kernel, alpha=alpha, tiles_per_seq=tiles_per_seq),
        grid=(m // tm, n_c),
        in_specs=[
            pl.BlockSpec((tm, d), lambda i, c: (i, 0)),
            pl.BlockSpec((None, SUBLANES, tn), lambda i, c: (i // tiles_per_seq, 0, c)),
            w_spec(0), w_spec(1), w_spec(2),
            pl.BlockSpec((None, 3, tn), lambda i, c: (j, 0, c)),
            pl.BlockSpec((None, tn, d), lambda i, c: (j, c, 0)),
            pl.BlockSpec((None, None, 1, d), lambda i, c: (layer, 1, 0, 0)),
            pl.BlockSpec((None, None, 1, d), lambda i, c: (layer, 1, 0, 0)),
        ],
        out_specs=[
            pl.BlockSpec((tm, d), lambda i, c: (i, 0)),
            pl.BlockSpec((None, SUBLANES, tn), lambda i, c: (i, 0, c)),
        ],
        out_shape=[jax.ShapeDtypeStruct((m, d), F32),
                   jax.ShapeDtypeStruct((m // tm, SUBLANES, dc), F32)],
        scratch_shapes=[pltpu.VMEM((tm, d), BF16), pltpu.VMEM((tm, d), F32),
                        pltpu.VMEM((n_c, SUBLANES, tn), F32)],
        compiler_params=_params("arbitrary", "arbitrary"),
        name="conv_mixer_seq",
    )(x, prev, w_in, w_in, w_in, w_conv, w_out, g, b)


def _conv_mixer_rows(x, p1, p2, f1, f2, w_in, w_conv, w_out, g, b, j, layer, alpha, tn_pref=512):
    m, d = x.shape
    dc = w_out.shape[1]
    tn = _tile(dc, tn_pref)
    n_c = dc // tn
    w_spec = lambda part: pl.BlockSpec((None, d, tn), lambda c: (j, 0, part * n_c + c))
    return pl.pallas_call(
        functools.partial(_conv_rows_kernel, alpha=alpha),
        grid=(n_c,),
        in_specs=[
            pl.BlockSpec((m, d), lambda c: (0, 0)),
            pl.BlockSpec((m, tn), lambda c: (0, c)),
            pl.BlockSpec((m, tn), lambda c: (0, c)),
            pl.BlockSpec((m, 1), lambda c: (0, 0)),
            pl.BlockSpec((m, 1), lambda c: (0, 0)),
            w_spec(0), w_spec(1), w_spec(2),
            pl.BlockSpec((None, 3, tn), lambda c: (j, 0, c)),
            pl.BlockSpec((None, tn, d), lambda c: (j, c, 0)),
            pl.BlockSpec((None, None, 1, d), lambda c: (layer, 1, 0, 0)),
            pl.BlockSpec((None, None, 1, d), lambda c: (layer, 1, 0, 0)),
        ],
        out_specs=[
            pl.BlockSpec((m, d), lambda c: (0, 0)),
            pl.BlockSpec((m, tn), lambda c: (0, c)),
        ],
        out_shape=[jax.ShapeDtypeStruct((m, d), F32), jax.ShapeDtypeStruct((m, dc), F32)],
        scratch_shapes=[pltpu.VMEM((m, d), BF16), pltpu.VMEM((m, d), F32)],
        compiler_params=_params("arbitrary"),
        name="conv_mixer_rows",
    )(x, p1, p2, f1, f2, w_in, w_in, w_in, w_conv, w_out, g, b)


def _mla_proj_kernel(x_ref, cos_ref, sin_ref, wqa_ref, gqa_ref, wqn_ref, wqr_ref, wqrs_ref, wkv_ref, gkv_ref,
                     wuk_ref, wuv_ref, qc_ref, kc_ref, v_ref, ckv_ref, kr_ref, *, n_heads, kv_lora, rope_dim):
    half = LANES // 2
    assert rope_dim == half, "rope halves are packed two heads per 128 lanes"
    xb = x_ref[...].astype(BF16)
    cos, sin = cos_ref[...], sin_ref[...]
    lane = lax.broadcasted_iota(jnp.int32, cos.shape, 1)
    low = lane < half

    cq = _rms_norm(_dot(xb, wqa_ref[...]), gqa_ref[...]).astype(BF16)
    qn = _dot(cq, wqn_ref[...])
    qr_a = _dot(cq, wqr_ref[...])
    qr_b = _dot(cq, wqrs_ref[...])
    for pair in range(n_heads // 2):
        sl = slice(pair * LANES, (pair + 1) * LANES)
        both = qr_a[:, sl] * cos + qr_b[:, sl] * sin
        for k in range(2):
            h = 2 * pair + k
            part = both if k == 0 else pltpu.roll(both, half, 1)
            qc_ref[:, h * 2 * LANES:h * 2 * LANES + LANES] = qn[:, h * LANES:(h + 1) * LANES].astype(BF16)
            qc_ref[:, h * 2 * LANES + LANES:(h + 1) * 2 * LANES] = jnp.where(low, part, 0.0).astype(BF16)

    kv = _dot(xb, wkv_ref[...])
    ckv = _rms_norm(kv[:, :kv_lora], gkv_ref[...])
    ckv_ref[...] = ckv
    kr = kv[:, kv_lora:kv_lora + LANES] * cos + kv[:, kv_lora + LANES:] * sin
    kr = jnp.where(low, kr, 0.0)
    kr_ref[...] = kr[:, :rope_dim]
    ckv_b = ckv.astype(BF16)
    kn = _dot(ckv_b, wuk_ref[...])
    v_ref[...] = _dot(ckv_b, wuv_ref[...]).astype(BF16)
    kr_b = kr.astype(BF16)
    for h in range(n_heads):
        kc_ref[:, h * 2 * LANES:h * 2 * LANES + LANES] = kn[:, h * LANES:(h + 1) * LANES].astype(BF16)
        kc_ref[:, h * 2 * LANES + LANES:(h + 1) * 2 * LANES] = kr_b


def _mla_proj(x, cos, sin, wqa, gqa, wqn, wqr, wqrs, wkv, gkv, wuk, wuv, j, n_heads, tm_pref=256):
    m, d = x.shape
    q_lora, kv_lora = wqa.shape[-1], wuk.shape[1]
    rope_dim = wqr.shape[-1] // n_heads
    tm = _tile(m, tm_pref)
    hq = n_heads * 2 * LANES
    hv = wuv.shape[-1]
    row = lambda w: pl.BlockSpec((tm, w), lambda i: (i, 0))
    full = lambda a: _resident((None,) + a.shape[1:], lambda i: (j,) + (0,) * (a.ndim - 1))
    return pl.pallas_call(
        functools.partial(_mla_proj_kernel, n_heads=n_heads, kv_lora=kv_lora, rope_dim=rope_dim),
        grid=(m // tm,),
        in_specs=[row(d), row(LANES), row(LANES), full(wqa), full(gqa), full(wqn), full(wqr), full(wqrs),
                  full(wkv), full(gkv), full(wuk), full(wuv)],
        out_specs=[row(hq), row(hq), row(hv), row(kv_lora), row(rope_dim)],
        out_shape=[jax.ShapeDtypeStruct((m, hq), BF16), jax.ShapeDtypeStruct((m, hq), BF16),
                   jax.ShapeDtypeStruct((m, hv), BF16), jax.ShapeDtypeStruct((m, kv_lora), F32),
                   jax.ShapeDtypeStruct((m, rope_dim), F32)],
        compiler_params=_params("parallel"),
        name="mla_proj",
    )(x, cos, sin, wqa, gqa, wqn, wqr, wqrs, wkv, gkv, wuk, wuv)


def _softmax_step(s, v, m_ref, l_ref, acc_ref):
    m_old = m_ref[...]
    m_new = jnp.maximum(m_old, jnp.max(s, axis=-1, keepdims=True))
    a = jnp.exp(m_old - m_new)
    p = jnp.exp(s - m_new)
    l_ref[...] = a * l_ref[...] + jnp.sum(p, axis=-1, keepdims=True)
    acc_ref[...] = a * acc_ref[...] + _dot(p.astype(BF16), v)
    m_ref[...] = m_new


def _attn_kernel(*refs, scale, tk, has_prefix):
    if has_prefix:
        q_ref, k_ref, v_ref, kp_ref, vp_ref, o_ref, m_ref, l_ref, acc_ref = refs
    else:
        q_ref, k_ref, v_ref, o_ref, m_ref, l_ref, acc_ref = refs
    qi = pl.program_id(2)
    q = q_ref[...]
    tq = q.shape[0]
    assert tq == tk

    m_ref[...] = jnp.full_like(m_ref, -jnp.inf)
    l_ref[...] = jnp.zeros_like(l_ref)
    acc_ref[...] = jnp.zeros_like(acc_ref)
    if has_prefix:
        _softmax_step(_dot_nt(q, kp_ref[...]) * scale, vp_ref[...], m_ref, l_ref, acc_ref)

    def body(jb, carry):
        start = pl.multiple_of(jb * tk, tk)
        s = _dot_nt(q, k_ref[pl.ds(start, tk), :]) * scale
        _softmax_step(s, v_ref[pl.ds(start, tk), :], m_ref, l_ref, acc_ref)
        return carry

    lax.fori_loop(0, qi, body, 0)

    start = pl.multiple_of(qi * tk, tk)
    s = _dot_nt(q, k_ref[pl.ds(start, tk), :]) * scale
    row = lax.broadcasted_iota(jnp.int32, s.shape, 0)
    col = lax.broadcasted_iota(jnp.int32, s.shape, 1)
    s = jnp.where(col <= row, s, -jnp.inf)
    _softmax_step(s, v_ref[pl.ds(start, tk), :], m_ref, l_ref, acc_ref)
    o_ref[...] = (acc_ref[...] / l_ref[...]).astype(o_ref.dtype)


def _attention(q, k, v, n_seq, seq_len, row0, n_heads, scale, prefix=None, t_pref=512):
    hv = v.shape[1] // n_heads
    t = _tile(seq_len, t_pref)
    n_t = seq_len // t
    assert row0 % t == 0 and row0 % seq_len == 0
    qblk0, sblk0 = row0 // t, row0 // seq_len
    in_specs = [
        pl.BlockSpec((t, 2 * LANES), lambda b, h, i: (qblk0 + b * n_t + i, h)),
        pl.BlockSpec((seq_len, 2 * LANES), lambda b, h, i: (sblk0 + b, h)),
        pl.BlockSpec((seq_len, hv), lambda b, h, i: (sblk0 + b, h)),
    ]
    args = [q, k, v]
    if prefix is not None:
        k_pre, v_pre, row0_pre, n_pre = prefix
        assert row0_pre % n_pre == 0
        pblk = row0_pre // n_pre
        in_specs += [pl.BlockSpec((n_pre, 2 * LANES), lambda b, h, i: (pblk, h)),
                     pl.BlockSpec((n_pre, hv), lambda b, h, i: (pblk, h))]
        args += [k_pre, v_pre]
    return pl.pallas_call(
        functools.partial(_attn_kernel, scale=scale, tk=t, has_prefix=prefix is not None),
        grid=(n_seq, n_heads, n_t),
        in_specs=in_specs,
        out_specs=pl.BlockSpec((t, hv), lambda b, h, i: (b * n_t + i, h)),
        out_shape=jax.ShapeDtypeStruct((n_seq * seq_len, n_heads * hv), BF16),
        scratch_shapes=[pltpu.VMEM((t, 1), F32), pltpu.VMEM((t, 1), F32), pltpu.VMEM((t, hv), F32)],
        compiler_params=_params("parallel", "parallel", "arbitrary"),
        name="attn_causal",
    )(*args)


def _absorb_q_kernel(q_ref, wuk_ref, o_ref):
    o_ref[...] = _dot_nt(q_ref[:, :LANES], wuk_ref[...]).astype(o_ref.dtype)


def _absorb_q(q_cat, wuk, j, n_rows, n_heads):
    kv_lora = wuk.shape[1]
    return pl.pallas_call(
        _absorb_q_kernel,
        grid=(n_heads,),
        in_specs=[pl.BlockSpec((n_rows, 2 * LANES), lambda h: (0, h)),
                  pl.BlockSpec((None, kv_lora, LANES), lambda h: (j, 0, h))],
        out_specs=pl.BlockSpec((None, n_rows, kv_lora), lambda h: (h, 0, 0)),
        out_shape=jax.ShapeDtypeStruct((n_heads, n_rows, kv_lora), BF16),
        compiler_params=_params("parallel"),
        name="absorb_q",
    )(q_cat, wuk)


def _expand_o_kernel(o_ref, wuv_ref, out_ref):
    out_ref[...] = _dot(o_ref[...], wuv_ref[...]).astype(out_ref.dtype)


def _expand_o(o_lat, wuv, j, n_heads):
    _, n_rows, kv_lora = o_lat.shape
    hv = wuv.shape[-1] // n_heads
    return pl.pallas_call(
        _expand_o_kernel,
        grid=(n_heads,),
        in_specs=[pl.BlockSpec((None, n_rows, kv_lora), lambda h: (h, 0, 0)),
                  pl.BlockSpec((None, kv_lora, hv), lambda h: (j, 0, h))],
        out_specs=pl.BlockSpec((n_rows, hv), lambda h: (0, h)),
        out_shape=jax.ShapeDtypeStruct((n_rows, n_heads * hv), BF16),
        compiler_params=_params("parallel"),
        name="expand_o",
    )(o_lat, wuv)


def _sample_attn_kernel(pt_ref, ql_ref, qr_ref, cnew_ref, krnew_ref, *refs, scale, group):
    del pt_ref
    ckv_refs, kr_refs = refs[:group], refs[group:2 * group]
    o_ref, m_ref, l_ref, acc_ref = refs[2 * group:]
    c = pl.program_id(1)
    ql, qr = ql_ref[...], qr_ref[...]

    @pl.when(c == 0)
    def _():
        cn = cnew_ref[...].astype(BF16).astype(F32)
        kn = krnew_ref[...].astype(BF16).astype(F32)
        s_new = (jnp.sum(ql.astype(F32) * cn, axis=-1, keepdims=True)
                 + jnp.sum(qr.astype(F32) * kn, axis=-1, keepdims=True)) * scale
        m_ref[...] = s_new
        l_ref[...] = jnp.ones_like(l_ref)
        acc_ref[...] = jnp.broadcast_to(cn, acc_ref.shape)

    pages = [r[...].astype(BF16) for r in ckv_refs]
    s = jnp.concatenate(
        [_dot_nt(ql, pages[g]) + _dot_nt(qr, kr_refs[g][...].astype(BF16)) for g in range(group)],
        axis=1) * scale
    m_old = m_ref[...]
    m_new = jnp.maximum(m_old, jnp.max(s, axis=-1, keepdims=True))
    a = jnp.exp(m_old - m_new)
    p = jnp.exp(s - m_new)
    l_ref[...] = a * l_ref[...] + jnp.sum(p, axis=-1, keepdims=True)
    pb = p.astype(BF16)
    page = pages[0].shape[0]
    pv = _dot(pb[:, :page], pages[0])
    for g in range(1, group):
        pv += _dot(pb[:, g * page:(g + 1) * page], pages[g])
    acc_ref[...] = a * acc_ref[...] + pv
    m_ref[...] = m_new

    @pl.when(c == pl.num_programs(1) - 1)
    def _():
        o_ref[...] = (acc_ref[...] / l_ref[...]).astype(o_ref.dtype)


def _sample_attention(q_lat, q_rope, c_new, kr_new, cache_ckv, cache_kr, page_table, j, scale, group_pref=8):
    n, n_heads, kv_lora = q_lat.shape
    rope_dim = q_rope.shape[-1]
    n_pages, page = page_table.shape[1], cache_ckv.shape[2]
    group = max(g for g in range(1, group_pref + 1) if n_pages % g == 0)
    pt_flat = page_table.reshape(-1)

    def page_spec(width, g):
        return pl.BlockSpec((None, None, page, width),
                            lambda b, c, pt: (j, pt[b * n_pages + c * group + g], 0, 0))

    per_row = lambda s1, w: pl.BlockSpec((None, s1, w), lambda b, c, pt: (b, 0, 0))
    grid_spec = pltpu.PrefetchScalarGridSpec(
        num_scalar_prefetch=1,
        grid=(n, n_pages // group),
        in_specs=[per_row(n_heads, kv_lora), per_row(n_heads, rope_dim), per_row(1, kv_lora), per_row(1, rope_dim)]
        + [page_spec(kv_lora, g) for g in range(group)] + [page_spec(rope_dim, g) for g in range(group)],
        out_specs=per_row(n_heads, kv_lora),
        scratch_shapes=[pltpu.VMEM((n_heads, 1), F32), pltpu.VMEM((n_heads, 1), F32),
                        pltpu.VMEM((n_heads, kv_lora), F32)],
    )
    return pl.pallas_call(
        functools.partial(_sample_attn_kernel, scale=scale, group=group),
        grid_spec=grid_spec,
        out_shape=jax.ShapeDtypeStruct((n, n_heads, kv_lora), BF16),
        compiler_params=_params("parallel", "arbitrary"),
        name="sample_attn",
    )(pt_flat, q_lat, q_rope, c_new, kr_new, *([cache_ckv] * group), *([cache_kr] * group))


def _proj_ln_kernel(a_ref, x_ref, w_ref, g_ref, b_ref, o_ref, *, alpha):
    y = alpha * x_ref[...] + _dot(a_ref[...], w_ref[...])
    o_ref[...] = _layer_norm(y, g_ref[...], b_ref[...])


def _proj_ln(a, x, w, g, b, j, layer, alpha, tm_pref=512):
    m, d = x.shape
    k = a.shape[1]
    tm = _tile(m, tm_pref)
    return pl.pallas_call(
        functools.partial(_proj_ln_kernel, alpha=alpha),
        grid=(m // tm,),
        in_specs=[
            pl.BlockSpec((tm, k), lambda i: (i, 0)),
            pl.BlockSpec((tm, d), lambda i: (i, 0)),
            _resident((None, k, d), lambda i: (j, 0, 0)),
            pl.BlockSpec((None, None, 1, d), lambda i: (layer, 1, 0, 0)),
            pl.BlockSpec((None, None, 1, d), lambda i: (layer, 1, 0, 0)),
        ],
        out_specs=pl.BlockSpec((tm, d), lambda i: (i, 0)),
        out_shape=jax.ShapeDtypeStruct((m, d), F32),
        compiler_params=_params("parallel"),
        name="proj_ln",
    )(a, x, w, g, b)


def _rope_tables(pos, rope_dim):
    inv = ROPE_THETA ** (-jnp.arange(0, rope_dim, 2, dtype=F32) / rope_dim)
    ang = pos.astype(F32)[:, None] * inv[None, :]
    cos, sin = jnp.cos(ang), jnp.sin(ang)
    reps = LANES // rope_dim
    return jnp.tile(jnp.concatenate([cos, cos], -1), (1, reps)), jnp.tile(jnp.concatenate([-sin, sin], -1), (1, reps))


def _swap_halves(w):
    half = w.shape[-1] // 2
    return jnp.concatenate([w[..., half:], w[..., :half]], axis=-1)


def kernel(x_prompt, x_sample, state_conv, cache_kv_latent, cache_k_rope, page_table, meta_tokens, ln_g, ln_b, w_ffn_gate, w_ffn_up, w_ffn_down, w_conv_in, w_conv, w_conv_out, w_q_a, g_q_a, w_q_b, w_kv_a, g_kv_a, w_uk, w_uv, w_o):
    nb, seq, d = x_prompt.shape
    ns, dec_seq, _ = x_sample.shape
    assert dec_seq == 1, "the sample path handles one new token per sequence"
    n_meta = meta_tokens.shape[0]
    depth = ln_g.shape[0]
    alpha = float((2 * depth) ** 0.25)
    n_heads = w_q_b.shape[2]
    rope_dim = cache_k_rope.shape[-1]
    nope_dim = w_q_b.shape[3] - rope_dim
    kv_lora = w_uk.shape[1]
    scale = float((nope_dim + rope_dim) ** -0.5)
    past = page_table.shape[1] * cache_kv_latent.shape[2]
    assert nope_dim == LANES and w_uv.shape[-1] == LANES and n_heads % 2 == 0

    xm = x_prompt.reshape(nb * seq, d)
    xe = jnp.concatenate([x_sample.reshape(ns, d), meta_tokens.astype(x_prompt.dtype)], axis=0)
    n_ext = ns + n_meta

    wg, wu, wd = w_ffn_gate.astype(BF16), w_ffn_up.astype(BF16), w_ffn_down.astype(BF16)
    wci, wco = w_conv_in.astype(BF16), w_conv_out.astype(BF16)
    wqa = w_q_a.astype(BF16)
    wqn = w_q_b[..., :nope_dim].reshape(w_q_b.shape[0], w_q_b.shape[1], n_heads * nope_dim).astype(BF16)
    wq_rope = w_q_b[..., nope_dim:]
    wqr = wq_rope.reshape(w_q_b.shape[0], w_q_b.shape[1], n_heads * rope_dim).astype(BF16)
    wqrs = _swap_halves(wq_rope).reshape(wqr.shape).astype(BF16)
    wk_rope = w_kv_a[..., kv_lora:]
    zpad = jnp.zeros(wk_rope.shape[:-1] + (LANES - rope_dim,), w_kv_a.dtype)
    wkv = jnp.concatenate([w_kv_a[..., :kv_lora], wk_rope, zpad, _swap_halves(wk_rope), zpad], axis=-1).astype(BF16)
    wuk = w_uk.reshape(w_uk.shape[0], kv_lora, n_heads * nope_dim).astype(BF16)
    wuv = w_uv.reshape(w_uv.shape[0], kv_lora, -1).astype(BF16)
    wo = w_o.astype(BF16)
    ln_g4, ln_b4 = ln_g[:, :, None, :], ln_b[:, :, None, :]
    g_q3, g_kv3 = g_q_a[:, None, :], g_kv_a[:, None, :]

    cos_m, sin_m = _rope_tables(jnp.tile(n_meta + jnp.arange(seq), nb), rope_dim)
    cos_e, sin_e = _rope_tables(jnp.concatenate([jnp.full((ns,), past), jnp.arange(n_meta)]), rope_dim)

    conv_p, conv_s, ckv_p, ckv_s, kr_p, kr_s = [], [], [], [], [], []
    for i in range(depth):
        j = i // N_MIXERS
        xm = _ffn(xm, wg, wu, wd, ln_g4, ln_b4, i, 0, 0, alpha)
        xe = _ffn(xe, wg, wu, wd, ln_g4, ln_b4, i, 0, 0, alpha)
        if i % N_MIXERS == 0:
            dc = w_conv_out.shape[1]
            st = state_conv[j]
            zeros = jnp.zeros((n_meta, dc), F32)
            p1 = jnp.concatenate([st[:, 1], zeros], axis=0)
            p2 = jnp.concatenate([st[:, 0], zeros], axis=0)
            rows = jnp.arange(n_ext)[:, None]
            f1 = (rows <= ns).astype(F32)
            f2 = (rows <= ns + 1).astype(F32)
            xe, u_e = _conv_mixer_rows(xe, p1, p2, f1, f2, wci, w_conv, wco, ln_g4, ln_b4, j, i, alpha)
            prev = jnp.broadcast_to(jnp.pad(u_e[n_ext - 2:], ((SUBLANES - 2, 0), (0, 0)))[None], (nb, SUBLANES, dc))
            xm, u_last = _conv_mixer_seq(xm, prev, wci, w_conv, wco, ln_g4, ln_b4, j, i, alpha, seq)
            tiles_per_seq = u_last.shape[0] // nb
            conv_p.append(u_last[tiles_per_seq - 1::tiles_per_seq, SUBLANES - 2:])
            conv_s.append(jnp.stack([st[:, 1], u_e[:ns]], axis=1))
        else:
            proj = functools.partial(_mla_proj, wqa=wqa, gqa=g_q3, wqn=wqn, wqr=wqr, wqrs=wqrs, wkv=wkv, gkv=g_kv3,
                                     wuk=wuk, wuv=wuv, j=j, n_heads=n_heads)
            qc_e, kc_e, v_e, ckv_e, kr_e = proj(xe, cos_e, sin_e)
            qc_m, kc_m, v_m, ckv_m, kr_m = proj(xm, cos_m, sin_m)
            o_meta = _attention(qc_e, kc_e, v_e, 1, n_meta, ns, n_heads, scale)
            o_main = _attention(qc_m, kc_m, v_m, nb, seq, 0, n_heads, scale, prefix=(kc_e, v_e, ns, n_meta))
            q_lat = _absorb_q(qc_e, wuk, j, ns, n_heads).transpose(1, 0, 2)
            q_rope_s = qc_e[:ns].reshape(ns, n_heads, 2 * LANES)[:, :, LANES:LANES + rope_dim]
            o_lat = _sample_attention(q_lat, q_rope_s, ckv_e[:ns, None], kr_e[:ns, None],
                                      cache_kv_latent, cache_k_rope, page_table, j, scale)
            o_s = _expand_o(o_lat.transpose(1, 0, 2), wuv, j, n_heads)
            o_e = jnp.concatenate([o_s, o_meta], axis=0)
            xe = _proj_ln(o_e, xe, wo, ln_g4, ln_b4, j, i, alpha)
            xm = _proj_ln(o_main, xm, wo, ln_g4, ln_b4, j, i, alpha)
            meta_rows = lambda a: jnp.broadcast_to(a[ns:][None], (nb, n_meta, a.shape[-1]))
            ckv_p.append(jnp.concatenate([meta_rows(ckv_e), ckv_m.reshape(nb, seq, kv_lora)], axis=1))
            kr_p.append(jnp.concatenate([meta_rows(kr_e), kr_m.reshape(nb, seq, rope_dim)], axis=1))
            ckv_s.append(ckv_e[:ns, None])
            kr_s.append(kr_e[:ns, None])
        xm = _ffn(xm, wg, wu, wd, ln_g4, ln_b4, i, 1, 2, alpha)
        xe = _ffn(xe, wg, wu, wd, ln_g4, ln_b4, i, 1, 2, alpha)

    y_prompt = xm.reshape(nb, seq, d)
    y_sample = xe[:ns].reshape(ns, dec_seq, d)
    return (y_prompt, y_sample, jnp.stack(conv_p), jnp.stack(conv_s), jnp.stack(ckv_p), jnp.stack(ckv_s),
            jnp.stack(kr_p), jnp.stack(kr_s))
```

```python
import functools

import jax
import jax.numpy as jnp
from jax import lax
from jax.experimental import pallas as pl
from jax.experimental.pallas import tpu as pltpu

LN_EPS = 1e-5
RMS_EPS = 1e-6
ROPE_THETA = 10000.0
N_MIXERS = 2
LOG2_E = 1.4426950408889634

LANES = 128
SUBLANES = 8
VMEM_LIMIT_BYTES = 56 * 1024 * 1024

BF16 = jnp.bfloat16
F32 = jnp.float32


def _tile(n, pref):
    if n <= pref:
        return n
    for t in range(pref, 15, -1):
        if n % t == 0 and t % 16 == 0:
            return t
    return n


def _params(*sem):
    return pltpu.CompilerParams(dimension_semantics=sem, vmem_limit_bytes=VMEM_LIMIT_BYTES)


def _resident(shape, index_map):
    return pl.BlockSpec(shape, index_map, pipeline_mode=pl.Buffered(1))


def _layer_norm(y, g, b):
    mu = jnp.mean(y, axis=-1, keepdims=True)
    yc = y - mu
    var = jnp.mean(yc * yc, axis=-1, keepdims=True)
    return yc * lax.rsqrt(var + LN_EPS) * g + b


def _rms_norm(y, g):
    return y * lax.rsqrt(jnp.mean(y * y, axis=-1, keepdims=True) + RMS_EPS) * g


def _dot(a, b):
    return jnp.dot(a, b, preferred_element_type=F32)


def _dot_nt(a, b):
    return lax.dot_general(a, b, (((1,), (1,)), ((), ())), preferred_element_type=F32)


def _ffn_step(f, n_f, x_ref, wg, wu, wd, g_ref, b_ref, o_ref, xb_ref, acc_ref, alpha):
    @pl.when(f == 0)
    def _():
        xb_ref[...] = x_ref[...].astype(BF16)
        acc_ref[...] = jnp.zeros_like(acc_ref)

    xb = xb_ref[...]
    gate = _dot(xb, wg)
    up = _dot(xb, wu)
    h = gate / (1.0 + jnp.exp(-gate)) * up
    acc_ref[...] += _dot(h.astype(BF16), wd)

    @pl.when(f == n_f - 1)
    def _():
        y = alpha * x_ref[...] + 0.5 * acc_ref[...]
        o_ref[...] = _layer_norm(y, g_ref[...], b_ref[...])


def _ffn_kernel(x_ref, wg_ref, wu_ref, wd_ref, g_ref, b_ref, o_ref, xb_ref, acc_ref, *, alpha):
    _ffn_step(pl.program_id(1), pl.num_programs(1), x_ref, wg_ref[...], wu_ref[...], wd_ref[...],
              g_ref, b_ref, o_ref, xb_ref, acc_ref, alpha)


def _ffn_cast_kernel(x_ref, wg_ref, wu_ref, wd_ref, g_ref, b_ref, o_ref, wgb_ref, wub_ref, wdb_ref,
                     xb_ref, acc_ref, *, alpha):
    wg, wu, wd = wg_ref[...].astype(BF16), wu_ref[...].astype(BF16), wd_ref[...].astype(BF16)
    wgb_ref[...] = wg
    wub_ref[...] = wu
    wdb_ref[...] = wd
    _ffn_step(pl.program_id(0), pl.num_programs(0), x_ref, wg, wu, wd, g_ref, b_ref, o_ref, xb_ref, acc_ref, alpha)


def _ffn(x, wg, wu, wd, g, b, layer, ln_idx, alpha, tm_pref=512, tf_pref=512):
    m, d = x.shape
    ff = wg.shape[-1]
    tm, tf = _tile(m, tm_pref), _tile(ff, tf_pref)
    return pl.pallas_call(
        functools.partial(_ffn_kernel, alpha=alpha),
        grid=(m // tm, ff // tf),
        in_specs=[
            pl.BlockSpec((tm, d), lambda i, f: (i, 0)),
            pl.BlockSpec((d, tf), lambda i, f: (0, f)),
            pl.BlockSpec((d, tf), lambda i, f: (0, f)),
            pl.BlockSpec((tf, d), lambda i, f: (f, 0)),
            pl.BlockSpec((None, None, 1, d), lambda i, f: (layer, ln_idx, 0, 0)),
            pl.BlockSpec((None, None, 1, d), lambda i, f: (layer, ln_idx, 0, 0)),
        ],
        out_specs=pl.BlockSpec((tm, d), lambda i, f: (i, 0)),
        out_shape=jax.ShapeDtypeStruct((m, d), F32),
        scratch_shapes=[pltpu.VMEM((tm, d), BF16), pltpu.VMEM((tm, d), F32)],
        compiler_params=_params("parallel", "arbitrary"),
        name="ffn_macaron",
    )(x, wg, wu, wd, g, b)


def _ffn_cast(x, wg, wu, wd, g, b, layer, half, ln_idx, alpha, tf_pref=256):
    m, d = x.shape
    ff = wg.shape[-1]
    tf = _tile(ff, tf_pref)
    return pl.pallas_call(
        functools.partial(_ffn_cast_kernel, alpha=alpha),
        grid=(ff // tf,),
        in_specs=[
            pl.BlockSpec((m, d), lambda f: (0, 0)),
            pl.BlockSpec((None, None, d, tf), lambda f: (layer, half, 0, f)),
            pl.BlockSpec((None, None, d, tf), lambda f: (layer, half, 0, f)),
            pl.BlockSpec((None, None, tf, d), lambda f: (layer, half, f, 0)),
            pl.BlockSpec((None, None, 1, d), lambda f: (layer, ln_idx, 0, 0)),
            pl.BlockSpec((None, None, 1, d), lambda f: (layer, ln_idx, 0, 0)),
        ],
        out_specs=[
            pl.BlockSpec((m, d), lambda f: (0, 0)),
            pl.BlockSpec((d, tf), lambda f: (0, f)),
            pl.BlockSpec((d, tf), lambda f: (0, f)),
            pl.BlockSpec((tf, d), lambda f: (f, 0)),
        ],
        out_shape=[jax.ShapeDtypeStruct((m, d), F32), jax.ShapeDtypeStruct((d, ff), BF16),
                   jax.ShapeDtypeStruct((d, ff), BF16), jax.ShapeDtypeStruct((ff, d), BF16)],
        scratch_shapes=[pltpu.VMEM((m, d), BF16), pltpu.VMEM((m, d), F32)],
        compiler_params=_params("arbitrary"),
        name="ffn_macaron_cast",
    )(x, wg, wu, wd, g, b)


def _conv_taps(u, u_m1, u_m2, wconv_ref):
    w = wconv_ref[...]
    return w[0:1] * u_m2 + w[1:2] * u_m1 + w[2:3] * u


def _conv_seq_kernel(x_ref, prev_ref, wb_ref, wc_ref, wh_ref, wconv_ref, wout_ref, g_ref, b_ref,
                     o_ref, ulast_ref, xb_ref, acc_ref, carry_ref, *, alpha, tiles_per_seq):
    i, c = pl.program_id(0), pl.program_id(1)
    tm = x_ref.shape[0]

    @pl.when(c == 0)
    def _():
        xb_ref[...] = x_ref[...].astype(BF16)
        acc_ref[...] = jnp.zeros_like(acc_ref)

    @pl.when(i == 0)
    def _():
        carry_ref[c] = jnp.zeros(carry_ref.shape[1:], F32)

    xb = xb_ref[...]
    u = _dot(xb, wc_ref[...]) * _dot(xb, wh_ref[...])
    seq_start = (i % tiles_per_seq) == 0
    tail = jnp.where(seq_start, prev_ref[...], carry_ref[c])
    row = lax.broadcasted_iota(jnp.int32, u.shape, 0)
    u_m1 = jnp.where(row == 0, tail[7:8], pltpu.roll(u, 1, 0))
    u_m2 = jnp.where(row == 0, tail[6:7], jnp.where(row == 1, tail[7:8], pltpu.roll(u, 2, 0)))
    conv = _conv_taps(u, u_m1, u_m2, wconv_ref)
    last = u[tm - SUBLANES:, :]
    carry_ref[c] = last
    ulast_ref[...] = last
    z = _dot(xb, wb_ref[...]) * conv
    acc_ref[...] += _dot(z.astype(BF16), wout_ref[...])

    @pl.when(c == pl.num_programs(1) - 1)
    def _():
        o_ref[...] = _layer_norm(alpha * x_ref[...] + acc_ref[...], g_ref[...], b_ref[...])


def _conv_rows_kernel(x_ref, p1_ref, p2_ref, f1_ref, f2_ref, wb_ref, wc_ref, wh_ref, wconv_ref, wout_ref,
                      g_ref, b_ref, o_ref, u_ref, wbb_ref, wcb_ref, whb_ref, woutb_ref, xb_ref, acc_ref, *, alpha):
    c = pl.program_id(0)

    @pl.when(c == 0)
    def _():
        xb_ref[...] = x_ref[...].astype(BF16)
        acc_ref[...] = jnp.zeros_like(acc_ref)

    wb, wc, wh = wb_ref[...].astype(BF16), wc_ref[...].astype(BF16), wh_ref[...].astype(BF16)
    wout = wout_ref[...].astype(BF16)
    wbb_ref[...] = wb
    wcb_ref[...] = wc
    whb_ref[...] = wh
    woutb_ref[...] = wout

    xb = xb_ref[...]
    u = _dot(xb, wc) * _dot(xb, wh)
    u_m1 = jnp.where(f1_ref[...] > 0.5, p1_ref[...], pltpu.roll(u, 1, 0))
    u_m2 = jnp.where(f2_ref[...] > 0.5, p2_ref[...], pltpu.roll(u, 2, 0))
    conv = _conv_taps(u, u_m1, u_m2, wconv_ref)
    u_ref[...] = u
    z = _dot(xb, wb) * conv
    acc_ref[...] += _dot(z.astype(BF16), wout)

    @pl.when(c == pl.num_programs(0) - 1)
    def _():
        o_ref[...] = _layer_norm(alpha * x_ref[...] + acc_ref[...], g_ref[...], b_ref[...])


def _conv_mixer_seq(x, prev, wb, wc, wh, w_conv, w_out, g, b, j, layer, alpha, seq_len, tm_pref=512, tn_pref=512):
    m, d = x.shape
    dc = w_out.shape[0]
    tm, tn = _tile(seq_len, tm_pref), _tile(dc, tn_pref)
    tiles_per_seq = seq_len // tm
    n_c = dc // tn
    w_spec = pl.BlockSpec((d, tn), lambda i, c: (0, c))
    return pl.pallas_call(
        functools.partial(_conv_seq_kernel, alpha=alpha, tiles_per_seq=tiles_per_seq),
        grid=(m // tm, n_c),
        in_specs=[
            pl.BlockSpec((tm, d), lambda i, c: (i, 0)),
            pl.BlockSpec((None, SUBLANES, tn), lambda i, c: (i // tiles_per_seq, 0, c)),
            w_spec, w_spec, w_spec,
            pl.BlockSpec((None, 3, tn), lambda i, c: (j, 0, c)),
            pl.BlockSpec((tn, d), lambda i, c: (c, 0)),
            pl.BlockSpec((None, None, 1, d), lambda i, c: (layer, 1, 0, 0)),
            pl.BlockSpec((None, None, 1, d), lambda i, c: (layer, 1, 0, 0)),
        ],
        out_specs=[
            pl.BlockSpec((tm, d), lambda i, c: (i, 0)),
            pl.BlockSpec((None, SUBLANES, tn), lambda i, c: (i, 0, c)),
        ],
        out_shape=[jax.ShapeDtypeStruct((m, d), F32),
                   jax.ShapeDtypeStruct((m // tm, SUBLANES, dc), F32)],
        scratch_shapes=[pltpu.VMEM((tm, d), BF16), pltpu.VMEM((tm, d), F32),
                        pltpu.VMEM((n_c, SUBLANES, tn), F32)],
        compiler_params=_params("arbitrary", "arbitrary"),
        name="conv_mixer_seq",
    )(x, prev, wb, wc, wh, w_conv, w_out, g, b)


def _conv_mixer_rows(x, p1, p2, f1, f2, w_in, w_conv, w_out, g, b, j, layer, alpha, tn_pref=256):
    m, d = x.shape
    dc = w_out.shape[1]
    tn = _tile(dc, tn_pref)
    n_c = dc // tn
    w_spec = lambda part: pl.BlockSpec((None, d, tn), lambda c: (j, 0, part * n_c + c))
    wb_out = pl.BlockSpec((d, tn), lambda c: (0, c))
    return pl.pallas_call(
        functools.partial(_conv_rows_kernel, alpha=alpha),
        grid=(n_c,),
        in_specs=[
            pl.BlockSpec((m, d), lambda c: (0, 0)),
            pl.BlockSpec((m, tn), lambda c: (0, c)),
            pl.BlockSpec((m, tn), lambda c: (0, c)),
            pl.BlockSpec((m, 1), lambda c: (0, 0)),
            pl.BlockSpec((m, 1), lambda c: (0, 0)),
            w_spec(0), w_spec(1), w_spec(2),
            pl.BlockSpec((None, 3, tn), lambda c: (j, 0, c)),
            pl.BlockSpec((None, tn, d), lambda c: (j, c, 0)),
            pl.BlockSpec((None, None, 1, d), lambda c: (layer, 1, 0, 0)),
            pl.BlockSpec((None, None, 1, d), lambda c: (layer, 1, 0, 0)),
        ],
        out_specs=[
            pl.BlockSpec((m, d), lambda c: (0, 0)),
            pl.BlockSpec((m, tn), lambda c: (0, c)),
            wb_out, wb_out, wb_out,
            pl.BlockSpec((tn, d), lambda c: (c, 0)),
        ],
        out_shape=[jax.ShapeDtypeStruct((m, d), F32), jax.ShapeDtypeStruct((m, dc), F32),
                   jax.ShapeDtypeStruct((d, dc), BF16), jax.ShapeDtypeStruct((d, dc), BF16),
                   jax.ShapeDtypeStruct((d, dc), BF16), jax.ShapeDtypeStruct((dc, d), BF16)],
        scratch_shapes=[pltpu.VMEM((m, d), BF16), pltpu.VMEM((m, d), F32)],
        compiler_params=_params("arbitrary"),
        name="conv_mixer_rows",
    )(x, p1, p2, f1, f2, w_in, w_in, w_in, w_conv, w_out, g, b)


def _mla_proj_kernel(x_ref, cos_ref, sin_ref, wqa_ref, gqa_ref, wqn_ref, wqr_ref, wqrs_ref, wkv_ref, gkv_ref,
                     wuk_ref, wuv_ref, qc_ref, kc_ref, v_ref, ckv_ref, kr_ref, *, n_heads, kv_lora, rope_dim):
    half = LANES // 2
    assert rope_dim == half, "rope halves are packed two heads per 128 lanes"
    xb = x_ref[...].astype(BF16)
    cos, sin = cos_ref[...], sin_ref[...]
    lane = lax.broadcasted_iota(jnp.int32, cos.shape, 1)
    low = lane < half

    cq = _rms_norm(_dot(xb, wqa_ref[...]), gqa_ref[...]).astype(BF16)
    qn = _dot(cq, wqn_ref[...])
    qr_a = _dot(cq, wqr_ref[...])
    qr_b = _dot(cq, wqrs_ref[...])
    for pair in range(n_heads // 2):
        sl = slice(pair * LANES, (pair + 1) * LANES)
        both = qr_a[:, sl] * cos + qr_b[:, sl] * sin
        for k in range(2):
            h = 2 * pair + k
            part = both if k == 0 else pltpu.roll(both, half, 1)
            qc_ref[:, h * 2 * LANES:h * 2 * LANES + LANES] = qn[:, h * LANES:(h + 1) * LANES].astype(BF16)
            qc_ref[:, h * 2 * LANES + LANES:(h + 1) * 2 * LANES] = jnp.where(low, part, 0.0).astype(BF16)

    kv = _dot(xb, wkv_ref[...])
    ckv = _rms_norm(kv[:, :kv_lora], gkv_ref[...])
    ckv_ref[...] = ckv
    kr = kv[:, kv_lora:kv_lora + LANES] * cos + kv[:, kv_lora + LANES:] * sin
    kr = jnp.where(low, kr, 0.0)
    kr_ref[...] = kr[:, :rope_dim]
    ckv_b = ckv.astype(BF16)
    kn = _dot(ckv_b, wuk_ref[...])
    v_ref[...] = _dot(ckv_b, wuv_ref[...]).astype(BF16)
    kr_b = kr.astype(BF16)
    for h in range(n_heads):
        kc_ref[:, h * 2 * LANES:h * 2 * LANES + LANES] = kn[:, h * LANES:(h + 1) * LANES].astype(BF16)
        kc_ref[:, h * 2 * LANES + LANES:(h + 1) * 2 * LANES] = kr_b


def _mla_proj(x, cos, sin, wqa, gqa, wqn, wqr, wqrs, wkv, gkv, wuk, wuv, j, n_heads, tm_pref=256):
    m, d = x.shape
    kv_lora = wuk.shape[1]
    rope_dim = wqr.shape[-1] // n_heads
    tm = _tile(m, tm_pref)
    hq = n_heads * 2 * LANES
    hv = wuv.shape[-1]
    row = lambda w: pl.BlockSpec((tm, w), lambda i: (i, 0))
    full = lambda a: _resident((None,) + a.shape[1:], lambda i: (j,) + (0,) * (a.ndim - 1))
    return pl.pallas_call(
        functools.partial(_mla_proj_kernel, n_heads=n_heads, kv_lora=kv_lora, rope_dim=rope_dim),
        grid=(m // tm,),
        in_specs=[row(d), row(LANES), row(LANES), full(wqa), full(gqa), full(wqn), full(wqr), full(wqrs),
                  full(wkv), full(gkv), full(wuk), full(wuv)],
        out_specs=[row(hq), row(hq), row(hv), row(kv_lora), row(rope_dim)],
        out_shape=[jax.ShapeDtypeStruct((m, hq), BF16), jax.ShapeDtypeStruct((m, hq), BF16),
                   jax.ShapeDtypeStruct((m, hv), BF16), jax.ShapeDtypeStruct((m, kv_lora), F32),
                   jax.ShapeDtypeStruct((m, rope_dim), F32)],
        compiler_params=_params("parallel"),
        name="mla_proj",
    )(x, cos, sin, wqa, gqa, wqn, wqr, wqrs, wkv, gkv, wuk, wuv)


def _lane_chunks(s):
    return [s[:, c * LANES:(c + 1) * LANES] for c in range(s.shape[1] // LANES)]


def _row_max(parts):
    m = functools.reduce(jnp.maximum, parts)
    return jnp.broadcast_to(jnp.max(m, axis=1, keepdims=True), m.shape)


def _attn_kernel(q_ref, k_ref, v_ref, kp_ref, vp_ref, o_ref, m_ref, l_ref, acc_ref, *, c_exp, tk, n_pre):
    qi = pl.program_id(2)
    dq = 2 * LANES
    n_grp = q_ref.shape[1] // dq
    assert q_ref.shape[0] == tk and v_ref.shape[1] == n_grp * LANES
    qs = [q_ref[:, h * dq:(h + 1) * dq] for h in range(n_grp)]
    k_blk = lambda h, start: k_ref[pl.ds(start, tk), h * dq:(h + 1) * dq]
    v_blk = lambda h, start: v_ref[pl.ds(start, tk), h * LANES:(h + 1) * LANES]
    hl = lambda h: slice(h * LANES, (h + 1) * LANES)

    start = pl.multiple_of(qi * tk, tk)
    row = lax.broadcasted_iota(jnp.int32, (tk, tk), 0)
    col = lax.broadcasted_iota(jnp.int32, (tk, tk), 1)
    causal = col <= row
    real_pre = lax.broadcasted_iota(jnp.int32, (tk, LANES), 1) < n_pre
    for h in range(n_grp):
        s_d = jnp.where(causal, _dot_nt(qs[h], k_blk(h, start)), -jnp.inf)
        s_p = jnp.where(real_pre, _dot_nt(qs[h], kp_ref[:, h * dq:(h + 1) * dq]), -jnp.inf)
        parts = _lane_chunks(s_d) + [s_p]
        m = _row_max(parts)
        p_parts = [jnp.exp2((x - m) * c_exp) for x in parts]
        m_ref[:, hl(h)] = m
        l_ref[:, hl(h)] = functools.reduce(jnp.add, p_parts)
        acc_ref[:, hl(h)] = (_dot(jnp.concatenate(p_parts[:-1], axis=1).astype(BF16), v_blk(h, start))
                             + _dot(p_parts[-1].astype(BF16), vp_ref[:, hl(h)]))

    def body(jb, carry):
        start = pl.multiple_of(jb * tk, tk)
        for h in range(n_grp):
            parts = _lane_chunks(_dot_nt(qs[h], k_blk(h, start)))
            m_old = m_ref[:, hl(h)]
            m_new = jnp.maximum(m_old, _row_max(parts))
            a = jnp.exp2((m_old - m_new) * c_exp)
            p_parts = [jnp.exp2((x - m_new) * c_exp) for x in parts]
            l_ref[:, hl(h)] = a * l_ref[:, hl(h)] + functools.reduce(jnp.add, p_parts)
            acc_ref[:, hl(h)] = a * acc_ref[:, hl(h)] + _dot(jnp.concatenate(p_parts, axis=1).astype(BF16),
                                                             v_blk(h, start))
            m_ref[:, hl(h)] = m_new
        return carry

    lax.fori_loop(0, qi, body, 0)
    for h in range(n_grp):
        l_row = jnp.sum(l_ref[:, hl(h)], axis=1, keepdims=True)
        o_ref[:, hl(h)] = (acc_ref[:, hl(h)] / l_row).astype(o_ref.dtype)


def _attention(q, k, v, k_pre, v_pre, n_pre, n_seq, seq_len, n_heads, scale, t_pref=512, heads_per_step=4):
    hv = v.shape[1] // n_heads
    t = _tile(seq_len, t_pref)
    n_t = seq_len // t
    g = max(x for x in range(1, heads_per_step + 1) if n_heads % x == 0)
    assert t % LANES == 0 and k_pre.shape[0] == LANES and hv == LANES
    return pl.pallas_call(
        functools.partial(_attn_kernel, c_exp=scale * LOG2_E, tk=t, n_pre=n_pre),
        grid=(n_seq, n_heads // g, n_t),
        in_specs=[
            pl.BlockSpec((t, g * 2 * LANES), lambda b, h, i: (b * n_t + i, h)),
            pl.BlockSpec((seq_len, g * 2 * LANES), lambda b, h, i: (b, h)),
            pl.BlockSpec((seq_len, g * hv), lambda b, h, i: (b, h)),
            pl.BlockSpec((LANES, g * 2 * LANES), lambda b, h, i: (0, h)),
            pl.BlockSpec((LANES, g * hv), lambda b, h, i: (0, h)),
        ],
        out_specs=pl.BlockSpec((t, g * hv), lambda b, h, i: (b * n_t + i, h)),
        out_shape=jax.ShapeDtypeStruct((n_seq * seq_len, n_heads * hv), BF16),
        scratch_shapes=[pltpu.VMEM((t, g * LANES), F32), pltpu.VMEM((t, g * LANES), F32),
                        pltpu.VMEM((t, g * hv), F32)],
        compiler_params=_params("parallel", "parallel", "arbitrary"),
        name="attn_causal",
    )(q, k, v, k_pre, v_pre)


def _prefix_attn_kernel(q_ref, k_ref, v_ref, o_ref, *, scale):
    s = _dot_nt(q_ref[...], k_ref[...]) * scale
    row = lax.broadcasted_iota(jnp.int32, s.shape, 0)
    col = lax.broadcasted_iota(jnp.int32, s.shape, 1)
    s = jnp.where(col <= row, s, -jnp.inf)
    p = jnp.exp(s - jnp.max(s, axis=1, keepdims=True))
    o = _dot(p.astype(BF16), v_ref[...]) / jnp.sum(p, axis=1, keepdims=True)
    o_ref[...] = o.astype(o_ref.dtype)


def _prefix_attention(q, k, v, n_heads, scale):
    n = q.shape[0]
    hv = v.shape[1] // n_heads
    return pl.pallas_call(
        functools.partial(_prefix_attn_kernel, scale=scale),
        grid=(n_heads,),
        in_specs=[pl.BlockSpec((n, 2 * LANES), lambda h: (0, h)),
                  pl.BlockSpec((n, 2 * LANES), lambda h: (0, h)),
                  pl.BlockSpec((n, hv), lambda h: (0, h))],
        out_specs=pl.BlockSpec((n, hv), lambda h: (0, h)),
        out_shape=jax.ShapeDtypeStruct((n, n_heads * hv), BF16),
        compiler_params=_params("parallel"),
        name="attn_prefix",
    )(q, k, v)


def _absorb_q_kernel(q_ref, wuk_ref, o_ref):
    o_ref[...] = _dot_nt(q_ref[:, :LANES], wuk_ref[...]).astype(o_ref.dtype)


def _absorb_q(q_cat, wuk, j, n_rows, n_heads):
    kv_lora = wuk.shape[1]
    return pl.pallas_call(
        _absorb_q_kernel,
        grid=(n_heads,),
        in_specs=[pl.BlockSpec((n_rows, 2 * LANES), lambda h: (0, h)),
                  pl.BlockSpec((None, kv_lora, LANES), lambda h: (j, 0, h))],
        out_specs=pl.BlockSpec((None, n_rows, kv_lora), lambda h: (h, 0, 0)),
        out_shape=jax.ShapeDtypeStruct((n_heads, n_rows, kv_lora), BF16),
        compiler_params=_params("parallel"),
        name="absorb_q",
    )(q_cat, wuk)


def _expand_o_kernel(o_ref, wuv_ref, out_ref):
    out_ref[...] = _dot(o_ref[...], wuv_ref[...]).astype(out_ref.dtype)


def _expand_o(o_lat, wuv, j, n_heads):
    _, n_rows, kv_lora = o_lat.shape
    hv = wuv.shape[-1] // n_heads
    return pl.pallas_call(
        _expand_o_kernel,
        grid=(n_heads,),
        in_specs=[pl.BlockSpec((None, n_rows, kv_lora), lambda h: (h, 0, 0)),
                  pl.BlockSpec((None, kv_lora, hv), lambda h: (j, 0, h))],
        out_specs=pl.BlockSpec((n_rows, hv), lambda h: (0, h)),
        out_shape=jax.ShapeDtypeStruct((n_rows, n_heads * hv), BF16),
        compiler_params=_params("parallel"),
        name="expand_o",
    )(o_lat, wuv)


def _sample_attn_kernel(pt_ref, ql_ref, qr_ref, cnew_ref, krnew_ref, *refs, scale, group):
    del pt_ref
    ckv_refs, krt_refs = refs[:group], refs[group:2 * group]
    o_ref, m_ref, l_ref, acc_ref = refs[2 * group:]
    c = pl.program_id(1)
    ql, qr = ql_ref[...], qr_ref[...]

    @pl.when(c == 0)
    def _():
        cn = cnew_ref[...].astype(BF16).astype(F32)
        kn = krnew_ref[...].astype(BF16).astype(F32)
        s_new = (jnp.sum(ql.astype(F32) * cn, axis=-1, keepdims=True)
                 + jnp.sum(qr.astype(F32) * kn, axis=-1, keepdims=True)) * scale
        m_ref[...] = s_new
        l_ref[...] = jnp.ones_like(l_ref)
        acc_ref[...] = jnp.broadcast_to(cn, acc_ref.shape)

    pages = [r[...].astype(BF16) for r in ckv_refs]
    s = jnp.concatenate(
        [_dot_nt(ql, pages[g]) + _dot(qr, krt_refs[g][...].astype(BF16)) for g in range(group)],
        axis=1) * scale
    m_old = m_ref[...]
    m_new = jnp.maximum(m_old, jnp.max(s, axis=-1, keepdims=True))
    a = jnp.exp(m_old - m_new)
    p = jnp.exp(s - m_new)
    l_ref[...] = a * l_ref[...] + jnp.sum(p, axis=-1, keepdims=True)
    pb = p.astype(BF16)
    page = pages[0].shape[0]
    pv = _dot(pb[:, :page], pages[0])
    for g in range(1, group):
        pv += _dot(pb[:, g * page:(g + 1) * page], pages[g])
    acc_ref[...] = a * acc_ref[...] + pv
    m_ref[...] = m_new

    @pl.when(c == pl.num_programs(1) - 1)
    def _():
        o_ref[...] = (acc_ref[...] / l_ref[...]).astype(o_ref.dtype)


def _sample_attention(q_lat, q_rope, c_new, kr_new, cache_ckv, cache_krt, page_table, j, scale, group_pref=64):
    n, n_heads, kv_lora = q_lat.shape
    rope_dim = q_rope.shape[-1]
    n_pages, page = page_table.shape[1], cache_ckv.shape[2]
    group = max(g for g in range(1, group_pref + 1) if n_pages % g == 0)
    pt_flat = page_table.reshape(-1)

    def page_spec(shape, g):
        return pl.BlockSpec((None, None) + shape, lambda b, c, pt: (j, pt[b * n_pages + c * group + g], 0, 0))

    per_row = lambda s1, w: pl.BlockSpec((None, s1, w), lambda b, c, pt: (b, 0, 0))
    grid_spec = pltpu.PrefetchScalarGridSpec(
        num_scalar_prefetch=1,
        grid=(n, n_pages // group),
        in_specs=[per_row(n_heads, kv_lora), per_row(n_heads, rope_dim), per_row(1, kv_lora), per_row(1, rope_dim)]
        + [page_spec((page, kv_lora), g) for g in range(group)]
        + [page_spec((rope_dim, page), g) for g in range(group)],
        out_specs=per_row(n_heads, kv_lora),
        scratch_shapes=[pltpu.VMEM((n_heads, 1), F32), pltpu.VMEM((n_heads, 1), F32),
                        pltpu.VMEM((n_heads, kv_lora), F32)],
    )
    return pl.pallas_call(
        functools.partial(_sample_attn_kernel, scale=scale, group=group),
        grid_spec=grid_spec,
        out_shape=jax.ShapeDtypeStruct((n, n_heads, kv_lora), BF16),
        compiler_params=_params("parallel", "arbitrary"),
        name="sample_attn",
    )(pt_flat, q_lat, q_rope, c_new, kr_new, *([cache_ckv] * group), *([cache_krt] * group))


def _proj_ln_kernel(a_ref, x_ref, w_ref, g_ref, b_ref, o_ref, *, alpha):
    y = alpha * x_ref[...] + _dot(a_ref[...], w_ref[...])
    o_ref[...] = _layer_norm(y, g_ref[...], b_ref[...])


def _proj_ln(a, x, w, g, b, j, layer, alpha, tm_pref=512):
    m, d = x.shape
    k = a.shape[1]
    tm = _tile(m, tm_pref)
    return pl.pallas_call(
        functools.partial(_proj_ln_kernel, alpha=alpha),
        grid=(m // tm,),
        in_specs=[
            pl.BlockSpec((tm, k), lambda i: (i, 0)),
            pl.BlockSpec((tm, d), lambda i: (i, 0)),
            _resident((None, k, d), lambda i: (j, 0, 0)),
            pl.BlockSpec((None, None, 1, d), lambda i: (layer, 1, 0, 0)),
            pl.BlockSpec((None, None, 1, d), lambda i: (layer, 1, 0, 0)),
        ],
        out_specs=pl.BlockSpec((tm, d), lambda i: (i, 0)),
        out_shape=jax.ShapeDtypeStruct((m, d), F32),
        compiler_params=_params("parallel"),
        name="proj_ln",
    )(a, x, w, g, b)


def _rope_tables(pos, rope_dim):
    inv = ROPE_THETA ** (-jnp.arange(0, rope_dim, 2, dtype=F32) / rope_dim)
    ang = pos.astype(F32)[:, None] * inv[None, :]
    cos, sin = jnp.cos(ang), jnp.sin(ang)
    reps = LANES // rope_dim
    return jnp.tile(jnp.concatenate([cos, cos], -1), (1, reps)), jnp.tile(jnp.concatenate([-sin, sin], -1), (1, reps))


def _swap_halves(w):
    half = w.shape[-1] // 2
    return jnp.concatenate([w[..., half:], w[..., :half]], axis=-1)


def _pad_rows(a, rows):
    return jnp.pad(a, ((0, rows - a.shape[0]), (0, 0)))


def kernel(x_prompt, x_sample, state_conv, cache_kv_latent, cache_k_rope, page_table, meta_tokens, ln_g, ln_b, w_ffn_gate, w_ffn_up, w_ffn_down, w_conv_in, w_conv, w_conv_out, w_q_a, g_q_a, w_q_b, w_kv_a, g_kv_a, w_uk, w_uv, w_o):
    nb, seq, d = x_prompt.shape
    ns, dec_seq, _ = x_sample.shape
    assert dec_seq == 1, "the sample path handles one new token per sequence"
    n_meta = meta_tokens.shape[0]
    depth = ln_g.shape[0]
    alpha = float((2 * depth) ** 0.25)
    n_heads = w_q_b.shape[2]
    rope_dim = cache_k_rope.shape[-1]
    nope_dim = w_q_b.shape[3] - rope_dim
    kv_lora = w_uk.shape[1]
    scale = float((nope_dim + rope_dim) ** -0.5)
    past = page_table.shape[1] * cache_kv_latent.shape[2]
    assert nope_dim == LANES and w_uv.shape[-1] == LANES and n_heads % 2 == 0 and n_meta <= LANES

    xm = x_prompt.reshape(nb * seq, d)
    xe = jnp.concatenate([x_sample.reshape(ns, d), meta_tokens.astype(x_prompt.dtype)], axis=0)
    n_ext = ns + n_meta

    wqa = w_q_a.astype(BF16)
    wqn = w_q_b[..., :nope_dim].reshape(w_q_b.shape[0], w_q_b.shape[1], n_heads * nope_dim).astype(BF16)
    wq_rope = w_q_b[..., nope_dim:]
    wqr = wq_rope.reshape(w_q_b.shape[0], w_q_b.shape[1], n_heads * rope_dim).astype(BF16)
    wqrs = _swap_halves(wq_rope).reshape(wqr.shape).astype(BF16)
    wk_rope = w_kv_a[..., kv_lora:]
    zpad = jnp.zeros(wk_rope.shape[:-1] + (LANES - rope_dim,), w_kv_a.dtype)
    wkv = jnp.concatenate([w_kv_a[..., :kv_lora], wk_rope, zpad, _swap_halves(wk_rope), zpad], axis=-1).astype(BF16)
    wuk = w_uk.reshape(w_uk.shape[0], kv_lora, n_heads * nope_dim).astype(BF16)
    wuv = w_uv.reshape(w_uv.shape[0], kv_lora, -1).astype(BF16)
    wo = w_o.astype(BF16)
    ln_g4, ln_b4 = ln_g[:, :, None, :], ln_b[:, :, None, :]
    g_q3, g_kv3 = g_q_a[:, None, :], g_kv_a[:, None, :]
    cache_krt = jnp.swapaxes(cache_k_rope, 2, 3)

    cos_m, sin_m = _rope_tables(jnp.tile(n_meta + jnp.arange(seq), nb), rope_dim)
    cos_e, sin_e = _rope_tables(jnp.concatenate([jnp.full((ns,), past), jnp.arange(n_meta)]), rope_dim)

    def ffn_both(xm, xe, layer, half, ln_idx):
        xe, wg, wu, wd = _ffn_cast(xe, w_ffn_gate, w_ffn_up, w_ffn_down, ln_g4, ln_b4, layer, half, ln_idx, alpha)
        return _ffn(xm, wg, wu, wd, ln_g4, ln_b4, layer, ln_idx, alpha), xe

    conv_p, conv_s, ckv_p, ckv_s, kr_p, kr_s = [], [], [], [], [], []
    for i in range(depth):
        j = i // N_MIXERS
        xm, xe = ffn_both(xm, xe, i, 0, 0)
        if i % N_MIXERS == 0:
            dc = w_conv_out.shape[1]
            st = state_conv[j]
            zeros = jnp.zeros((n_meta, dc), F32)
            p1 = jnp.concatenate([st[:, 1], zeros], axis=0)
            p2 = jnp.concatenate([st[:, 0], zeros], axis=0)
            rows = jnp.arange(n_ext)[:, None]
            f1 = (rows <= ns).astype(F32)
            f2 = (rows <= ns + 1).astype(F32)
            xe, u_e, wb, wc, wh, wout = _conv_mixer_rows(xe, p1, p2, f1, f2, w_conv_in, w_conv, w_conv_out,
                                                         ln_g4, ln_b4, j, i, alpha)
            prev = jnp.broadcast_to(jnp.pad(u_e[n_ext - 2:], ((SUBLANES - 2, 0), (0, 0)))[None], (nb, SUBLANES, dc))
            xm, u_last = _conv_mixer_seq(xm, prev, wb, wc, wh, w_conv, wout, ln_g4, ln_b4, j, i, alpha, seq)
            tiles_per_seq = u_last.shape[0] // nb
            conv_p.append(u_last[tiles_per_seq - 1::tiles_per_seq, SUBLANES - 2:])
            conv_s.append(jnp.stack([st[:, 1], u_e[:ns]], axis=1))
        else:
            proj = functools.partial(_mla_proj, wqa=wqa, gqa=g_q3, wqn=wqn, wqr=wqr, wqrs=wqrs, wkv=wkv, gkv=g_kv3,
                                     wuk=wuk, wuv=wuv, j=j, n_heads=n_heads)
            qc_e, kc_e, v_e, ckv_e, kr_e = proj(xe, cos_e, sin_e)
            qc_m, kc_m, v_m, ckv_m, kr_m = proj(xm, cos_m, sin_m)
            o_meta = _prefix_attention(qc_e[ns:], kc_e[ns:], v_e[ns:], n_heads, scale)
            o_main = _attention(qc_m, kc_m, v_m, _pad_rows(kc_e[ns:], LANES), _pad_rows(v_e[ns:], LANES), n_meta,
                                nb, seq, n_heads, scale)
            q_lat = _absorb_q(qc_e, wuk, j, ns, n_heads).transpose(1, 0, 2)
            q_rope_s = qc_e[:ns].reshape(ns, n_heads, 2 * LANES)[:, :, LANES:LANES + rope_dim]
            o_lat = _sample_attention(q_lat, q_rope_s, ckv_e[:ns, None], kr_e[:ns, None],
                                      cache_kv_latent, cache_krt, page_table, j, scale)
            o_s = _expand_o(o_lat.transpose(1, 0, 2), wuv, j, n_heads)
            o_e = jnp.concatenate([o_s, o_meta], axis=0)
            xe = _proj_ln(o_e, xe, wo, ln_g4, ln_b4, j, i, alpha)
            xm = _proj_ln(o_main, xm, wo, ln_g4, ln_b4, j, i, alpha)
            meta_rows = lambda a: jnp.broadcast_to(a[ns:][None], (nb, n_meta, a.shape[-1]))
            ckv_p.append(jnp.concatenate([meta_rows(ckv_e), ckv_m.reshape(nb, seq, kv_lora)], axis=1))
            kr_p.append(jnp.concatenate([meta_rows(kr_e), kr_m.reshape(nb, seq, rope_dim)], axis=1))
            ckv_s.append(ckv_e[:ns, None])
            kr_s.append(kr_e[:ns, None])
        xm, xe = ffn_both(xm, xe, i, 1, 2)

    y_prompt = xm.reshape(nb, seq, d)
    y_sample = xe[:ns].reshape(ns, dec_seq, d)
    return (y_prompt, y_sample, jnp.stack(conv_p), jnp.stack(conv_s), jnp.stack(ckv_p), jnp.stack(ckv_s),
            jnp.stack(kr_p), jnp.stack(kr_s))
```

```python
import functools

import jax
import jax.numpy as jnp
from jax import lax
from jax.experimental import pallas as pl
from jax.experimental.pallas import tpu as pltpu

LN_EPS = 1e-5
RMS_EPS = 1e-6
ROPE_THETA = 10000.0
N_MIXERS = 2
LOG2_E = 1.4426950408889634
FFN_SUB = 256

LANES = 128
SUBLANES = 8
VMEM_LIMIT_BYTES = 56 * 1024 * 1024

BF16 = jnp.bfloat16
F32 = jnp.float32


def _tile(n, pref):
    if n <= pref:
        return n
    for t in range(pref, 15, -1):
        if n % t == 0 and t % 16 == 0:
            return t
    return n


def _params(*sem):
    return pltpu.CompilerParams(dimension_semantics=sem, vmem_limit_bytes=VMEM_LIMIT_BYTES)


def _resident(shape, index_map):
    return pl.BlockSpec(shape, index_map, pipeline_mode=pl.Buffered(1))


def _layer_norm(y, g, b):
    mu = jnp.mean(y, axis=-1, keepdims=True)
    yc = y - mu
    var = jnp.mean(yc * yc, axis=-1, keepdims=True)
    return yc * lax.rsqrt(var + LN_EPS) * g + b


def _rms_norm(y, g):
    return y * lax.rsqrt(jnp.mean(y * y, axis=-1, keepdims=True) + RMS_EPS) * g


def _dot(a, b):
    return jnp.dot(a, b, preferred_element_type=F32)


def _dot_nt(a, b):
    return lax.dot_general(a, b, (((1,), (1,)), ((), ())), preferred_element_type=F32)


def _ffn_step(f, n_f, x_ref, wg_ref, wu_ref, wd_ref, g_ref, b_ref, o_ref, xb_ref, acc_ref, alpha):
    @pl.when(f == 0)
    def _():
        xb_ref[...] = x_ref[...].astype(BF16)
        acc_ref[...] = jnp.zeros_like(acc_ref)

    xb = xb_ref[...]
    tf = wg_ref.shape[1]
    sub = FFN_SUB if tf % FFN_SUB == 0 else tf
    part = None
    for c in range(tf // sub):
        cols = slice(c * sub, (c + 1) * sub)
        gate = _dot(xb, wg_ref[:, cols])
        up = _dot(xb, wu_ref[:, cols])
        h = gate / (1.0 + jnp.exp(-gate)) * up
        d = _dot(h.astype(BF16), wd_ref[cols, :])
        part = d if part is None else part + d
    acc_ref[...] += part

    @pl.when(f == n_f - 1)
    def _():
        y = alpha * x_ref[...] + 0.5 * acc_ref[...]
        o_ref[...] = _layer_norm(y, g_ref[...], b_ref[...])


def _ffn_kernel(x_ref, wg_ref, wu_ref, wd_ref, g_ref, b_ref, o_ref, xb_ref, acc_ref, *, alpha):
    _ffn_step(pl.program_id(1), pl.num_programs(1), x_ref, wg_ref, wu_ref, wd_ref,
              g_ref, b_ref, o_ref, xb_ref, acc_ref, alpha)


def _ffn_cast_kernel(x_ref, wg_ref, wu_ref, wd_ref, g_ref, b_ref, o_ref, wgb_ref, wub_ref, wdb_ref,
                     xb_ref, acc_ref, *, alpha):
    wgb_ref[...] = wg_ref[...].astype(BF16)
    wub_ref[...] = wu_ref[...].astype(BF16)
    wdb_ref[...] = wd_ref[...].astype(BF16)
    _ffn_step(pl.program_id(0), pl.num_programs(0), x_ref, wgb_ref, wub_ref, wdb_ref,
              g_ref, b_ref, o_ref, xb_ref, acc_ref, alpha)


def _ffn(x, wg, wu, wd, g, b, layer, ln_idx, alpha, tm_pref=512, tf_pref=512):
    m, d = x.shape
    ff = wg.shape[-1]
    tm, tf = _tile(m, tm_pref), _tile(ff, tf_pref)
    return pl.pallas_call(
        functools.partial(_ffn_kernel, alpha=alpha),
        grid=(m // tm, ff // tf),
        in_specs=[
            pl.BlockSpec((tm, d), lambda i, f: (i, 0)),
            pl.BlockSpec((d, tf), lambda i, f: (0, f)),
            pl.BlockSpec((d, tf), lambda i, f: (0, f)),
            pl.BlockSpec((tf, d), lambda i, f: (f, 0)),
            pl.BlockSpec((None, None, 1, d), lambda i, f: (layer, ln_idx, 0, 0)),
            pl.BlockSpec((None, None, 1, d), lambda i, f: (layer, ln_idx, 0, 0)),
        ],
        out_specs=pl.BlockSpec((tm, d), lambda i, f: (i, 0)),
        out_shape=jax.ShapeDtypeStruct((m, d), F32),
        scratch_shapes=[pltpu.VMEM((tm, d), BF16), pltpu.VMEM((tm, d), F32)],
        compiler_params=_params("parallel", "arbitrary"),
        name="ffn_macaron",
    )(x, wg, wu, wd, g, b)


def _ffn_cast(x, wg, wu, wd, g, b, layer, half, ln_idx, alpha, tf_pref=256):
    m, d = x.shape
    ff = wg.shape[-1]
    tf = _tile(ff, tf_pref)
    return pl.pallas_call(
        functools.partial(_ffn_cast_kernel, alpha=alpha),
        grid=(ff // tf,),
        in_specs=[
            pl.BlockSpec((m, d), lambda f: (0, 0)),
            pl.BlockSpec((None, None, d, tf), lambda f: (layer, half, 0, f)),
            pl.BlockSpec((None, None, d, tf), lambda f: (layer, half, 0, f)),
            pl.BlockSpec((None, None, tf, d), lambda f: (layer, half, f, 0)),
            pl.BlockSpec((None, None, 1, d), lambda f: (layer, ln_idx, 0, 0)),
            pl.BlockSpec((None, None, 1, d), lambda f: (layer, ln_idx, 0, 0)),
        ],
        out_specs=[
            pl.BlockSpec((m, d), lambda f: (0, 0)),
            pl.BlockSpec((d, tf), lambda f: (0, f)),
            pl.BlockSpec((d, tf), lambda f: (0, f)),
            pl.BlockSpec((tf, d), lambda f: (f, 0)),
        ],
        out_shape=[jax.ShapeDtypeStruct((m, d), F32), jax.ShapeDtypeStruct((d, ff), BF16),
                   jax.ShapeDtypeStruct((d, ff), BF16), jax.ShapeDtypeStruct((ff, d), BF16)],
        scratch_shapes=[pltpu.VMEM((m, d), BF16), pltpu.VMEM((m, d), F32)],
        compiler_params=_params("arbitrary"),
        name="ffn_macaron_cast",
    )(x, wg, wu, wd, g, b)


def _conv_taps(u, u_m1, u_m2, wconv_ref):
    w = wconv_ref[...]
    return w[0:1] * u_m2 + w[1:2] * u_m1 + w[2:3] * u


def _conv_seq_kernel(x_ref, prev_ref, wb_ref, wc_ref, wh_ref, wconv_ref, wout_ref, g_ref, b_ref,
                     o_ref, ulast_ref, xb_ref, acc_ref, carry_ref, *, alpha, tiles_per_seq):
    i, c = pl.program_id(0), pl.program_id(1)
    tm = x_ref.shape[0]

    @pl.when(c == 0)
    def _():
        xb_ref[...] = x_ref[...].astype(BF16)
        acc_ref[...] = jnp.zeros_like(acc_ref)

    @pl.when(i == 0)
    def _():
        carry_ref[c] = jnp.zeros(carry_ref.shape[1:], F32)

    xb = xb_ref[...]
    u = _dot(xb, wc_ref[...]) * _dot(xb, wh_ref[...])
    seq_start = (i % tiles_per_seq) == 0
    tail = jnp.where(seq_start, prev_ref[...], carry_ref[c])
    row = lax.broadcasted_iota(jnp.int32, u.shape, 0)
    u_m1 = jnp.where(row == 0, tail[7:8], pltpu.roll(u, 1, 0))
    u_m2 = jnp.where(row == 0, tail[6:7], jnp.where(row == 1, tail[7:8], pltpu.roll(u, 2, 0)))
    conv = _conv_taps(u, u_m1, u_m2, wconv_ref)
    last = u[tm - SUBLANES:, :]
    carry_ref[c] = last
    ulast_ref[...] = last
    z = _dot(xb, wb_ref[...]) * conv
    acc_ref[...] += _dot(z.astype(BF16), wout_ref[...])

    @pl.when(c == pl.num_programs(1) - 1)
    def _():
        o_ref[...] = _layer_norm(alpha * x_ref[...] + acc_ref[...], g_ref[...], b_ref[...])


def _conv_rows_kernel(x_ref, p1_ref, p2_ref, f1_ref, f2_ref, wb_ref, wc_ref, wh_ref, wconv_ref, wout_ref,
                      g_ref, b_ref, o_ref, u_ref, wbb_ref, wcb_ref, whb_ref, woutb_ref, xb_ref, acc_ref, *, alpha):
    c = pl.program_id(0)

    @pl.when(c == 0)
    def _():
        xb_ref[...] = x_ref[...].astype(BF16)
        acc_ref[...] = jnp.zeros_like(acc_ref)

    wb, wc, wh = wb_ref[...].astype(BF16), wc_ref[...].astype(BF16), wh_ref[...].astype(BF16)
    wout = wout_ref[...].astype(BF16)
    wbb_ref[...] = wb
    wcb_ref[...] = wc
    whb_ref[...] = wh
    woutb_ref[...] = wout

    xb = xb_ref[...]
    u = _dot(xb, wc) * _dot(xb, wh)
    u_m1 = jnp.where(f1_ref[...] > 0.5, p1_ref[...], pltpu.roll(u, 1, 0))
    u_m2 = jnp.where(f2_ref[...] > 0.5, p2_ref[...], pltpu.roll(u, 2, 0))
    conv = _conv_taps(u, u_m1, u_m2, wconv_ref)
    u_ref[...] = u
    z = _dot(xb, wb) * conv
    acc_ref[...] += _dot(z.astype(BF16), wout)

    @pl.when(c == pl.num_programs(0) - 1)
    def _():
        o_ref[...] = _layer_norm(alpha * x_ref[...] + acc_ref[...], g_ref[...], b_ref[...])


def _conv_mixer_seq(x, prev, wb, wc, wh, w_conv, w_out, g, b, j, layer, alpha, seq_len, tm_pref=512, tn_pref=512):
    m, d = x.shape
    dc = w_out.shape[0]
    tm, tn = _tile(seq_len, tm_pref), _tile(dc, tn_pref)
    tiles_per_seq = seq_len // tm
    n_c = dc // tn
    w_spec = pl.BlockSpec((d, tn), lambda i, c: (0, c))
    return pl.pallas_call(
        functools.partial(_conv_seq_kernel, alpha=alpha, tiles_per_seq=tiles_per_seq),
        grid=(m // tm, n_c),
        in_specs=[
            pl.BlockSpec((tm, d), lambda i, c: (i, 0)),
            pl.BlockSpec((None, SUBLANES, tn), lambda i, c: (i // tiles_per_seq, 0, c)),
            w_spec, w_spec, w_spec,
            pl.BlockSpec((None, 3, tn), lambda i, c: (j, 0, c)),
            pl.BlockSpec((tn, d), lambda i, c: (c, 0)),
            pl.BlockSpec((None, None, 1, d), lambda i, c: (layer, 1, 0, 0)),
            pl.BlockSpec((None, None, 1, d), lambda i, c: (layer, 1, 0, 0)),
        ],
        out_specs=[
            pl.BlockSpec((tm, d), lambda i, c: (i, 0)),
            pl.BlockSpec((None, SUBLANES, tn), lambda i, c: (i, 0, c)),
        ],
        out_shape=[jax.ShapeDtypeStruct((m, d), F32),
                   jax.ShapeDtypeStruct((m // tm, SUBLANES, dc), F32)],
        scratch_shapes=[pltpu.VMEM((tm, d), BF16), pltpu.VMEM((tm, d), F32),
                        pltpu.VMEM((n_c, SUBLANES, tn), F32)],
        compiler_params=_params("arbitrary", "arbitrary"),
        name="conv_mixer_seq",
    )(x, prev, wb, wc, wh, w_conv, w_out, g, b)


def _conv_mixer_rows(x, p1, p2, f1, f2, w_in, w_conv, w_out, g, b, j, layer, alpha, tn_pref=256):
    m, d = x.shape
    dc = w_out.shape[1]
    tn = _tile(dc, tn_pref)
    n_c = dc // tn
    w_spec = lambda part: pl.BlockSpec((None, d, tn), lambda c: (j, 0, part * n_c + c))
    wb_out = pl.BlockSpec((d, tn), lambda c: (0, c))
    return pl.pallas_call(
        functools.partial(_conv_rows_kernel, alpha=alpha),
        grid=(n_c,),
        in_specs=[
            pl.BlockSpec((m, d), lambda c: (0, 0)),
            pl.BlockSpec((m, tn), lambda c: (0, c)),
            pl.BlockSpec((m, tn), lambda c: (0, c)),
            pl.BlockSpec((m, 1), lambda c: (0, 0)),
            pl.BlockSpec((m, 1), lambda c: (0, 0)),
            w_spec(0), w_spec(1), w_spec(2),
            pl.BlockSpec((None, 3, tn), lambda c: (j, 0, c)),
            pl.BlockSpec((None, tn, d), lambda c: (j, c, 0)),
            pl.BlockSpec((None, None, 1, d), lambda c: (layer, 1, 0, 0)),
            pl.BlockSpec((None, None, 1, d), lambda c: (layer, 1, 0, 0)),
        ],
        out_specs=[
            pl.BlockSpec((m, d), lambda c: (0, 0)),
            pl.BlockSpec((m, tn), lambda c: (0, c)),
            wb_out, wb_out, wb_out,
            pl.BlockSpec((tn, d), lambda c: (c, 0)),
        ],
        out_shape=[jax.ShapeDtypeStruct((m, d), F32), jax.ShapeDtypeStruct((m, dc), F32),
                   jax.ShapeDtypeStruct((d, dc), BF16), jax.ShapeDtypeStruct((d, dc), BF16),
                   jax.ShapeDtypeStruct((d, dc), BF16), jax.ShapeDtypeStruct((dc, d), BF16)],
        scratch_shapes=[pltpu.VMEM((m, d), BF16), pltpu.VMEM((m, d), F32)],
        compiler_params=_params("arbitrary"),
        name="conv_mixer_rows",
    )(x, p1, p2, f1, f2, w_in, w_in, w_in, w_conv, w_out, g, b)


def _mla_proj_kernel(x_ref, cos_ref, sin_ref, wqa_ref, gqa_ref, wqn_ref, wqr_ref, wqrs_ref, wkv_ref, gkv_ref,
                     wuk_ref, wuv_ref, qc_ref, kc_ref, v_ref, ckv_ref, kr_ref, *, n_heads, kv_lora, rope_dim):
    half = LANES // 2
    assert rope_dim == half, "rope halves are packed two heads per 128 lanes"
    xb = x_ref[...].astype(BF16)
    cos, sin = cos_ref[...], sin_ref[...]
    lane = lax.broadcasted_iota(jnp.int32, cos.shape, 1)
    low = lane < half

    cq = _rms_norm(_dot(xb, wqa_ref[...]), gqa_ref[...]).astype(BF16)
    qn = _dot(cq, wqn_ref[...])
    qr_a = _dot(cq, wqr_ref[...])
    qr_b = _dot(cq, wqrs_ref[...])
    for pair in range(n_heads // 2):
        sl = slice(pair * LANES, (pair + 1) * LANES)
        both = qr_a[:, sl] * cos + qr_b[:, sl] * sin
        for k in range(2):
            h = 2 * pair + k
            part = both if k == 0 else pltpu.roll(both, half, 1)
            qc_ref[:, h * 2 * LANES:h * 2 * LANES + LANES] = qn[:, h * LANES:(h + 1) * LANES].astype(BF16)
            qc_ref[:, h * 2 * LANES + LANES:(h + 1) * 2 * LANES] = jnp.where(low, part, 0.0).astype(BF16)

    kv = _dot(xb, wkv_ref[...])
    ckv = _rms_norm(kv[:, :kv_lora], gkv_ref[...])
    ckv_ref[...] = ckv
    kr = kv[:, kv_lora:kv_lora + LANES] * cos + kv[:, kv_lora + LANES:] * sin
    kr = jnp.where(low, kr, 0.0)
    kr_ref[...] = kr[:, :rope_dim]
    ckv_b = ckv.astype(BF16)
    kn = _dot(ckv_b, wuk_ref[...])
    v = _dot(ckv_b, wuv_ref[...])
    kr_b = kr.astype(BF16)
    ones = jnp.ones((xb.shape[0], LANES), BF16)
    for h in range(n_heads):
        kc_ref[:, h * 2 * LANES:h * 2 * LANES + LANES] = kn[:, h * LANES:(h + 1) * LANES].astype(BF16)
        kc_ref[:, h * 2 * LANES + LANES:(h + 1) * 2 * LANES] = kr_b
        v_ref[:, h * 2 * LANES:h * 2 * LANES + LANES] = v[:, h * LANES:(h + 1) * LANES].astype(BF16)
        v_ref[:, h * 2 * LANES + LANES:(h + 1) * 2 * LANES] = ones


def _mla_proj(x, cos, sin, wqa, gqa, wqn, wqr, wqrs, wkv, gkv, wuk, wuv, j, n_heads, tm_pref=256):
    m, d = x.shape
    kv_lora = wuk.shape[1]
    rope_dim = wqr.shape[-1] // n_heads
    tm = _tile(m, tm_pref)
    hq = n_heads * 2 * LANES
    hv = hq
    row = lambda w: pl.BlockSpec((tm, w), lambda i: (i, 0))
    full = lambda a: _resident((None,) + a.shape[1:], lambda i: (j,) + (0,) * (a.ndim - 1))
    return pl.pallas_call(
        functools.partial(_mla_proj_kernel, n_heads=n_heads, kv_lora=kv_lora, rope_dim=rope_dim),
        grid=(m // tm,),
        in_specs=[row(d), row(LANES), row(LANES), full(wqa), full(gqa), full(wqn), full(wqr), full(wqrs),
                  full(wkv), full(gkv), full(wuk), full(wuv)],
        out_specs=[row(hq), row(hq), row(hv), row(kv_lora), row(rope_dim)],
        out_shape=[jax.ShapeDtypeStruct((m, hq), BF16), jax.ShapeDtypeStruct((m, hq), BF16),
                   jax.ShapeDtypeStruct((m, hv), BF16), jax.ShapeDtypeStruct((m, kv_lora), F32),
                   jax.ShapeDtypeStruct((m, rope_dim), F32)],
        compiler_params=_params("parallel"),
        name="mla_proj",
    )(x, cos, sin, wqa, gqa, wqn, wqr, wqrs, wkv, gkv, wuk, wuv)


def _lane_chunks(s):
    return [s[:, c * LANES:(c + 1) * LANES] for c in range(s.shape[1] // LANES)]


def _row_max(parts):
    m = functools.reduce(jnp.maximum, parts)
    return jnp.broadcast_to(jnp.max(m, axis=1, keepdims=True), m.shape)


def _attn_kernel(q_ref, k_ref, v_ref, kp_ref, vp_ref, o_ref, m_ref, acc_ref, *, c_exp, tk, n_pre):
    qi = pl.program_id(2)
    dq = 2 * LANES
    n_grp = q_ref.shape[1] // dq
    assert q_ref.shape[0] == tk and v_ref.shape[1] == n_grp * dq
    qs = [q_ref[:, h * dq:(h + 1) * dq] for h in range(n_grp)]
    hq = lambda h: slice(h * dq, (h + 1) * dq)
    k_blk = lambda h, start: k_ref[pl.ds(start, tk), hq(h)]
    v_blk = lambda h, start: v_ref[pl.ds(start, tk), hq(h)]
    hl = lambda h: slice(h * LANES, (h + 1) * LANES)

    start = pl.multiple_of(qi * tk, tk)
    row = lax.broadcasted_iota(jnp.int32, (tk, tk), 0)
    col = lax.broadcasted_iota(jnp.int32, (tk, tk), 1)
    causal = col <= row
    real_pre = lax.broadcasted_iota(jnp.int32, (tk, LANES), 1) < n_pre
    for h in range(n_grp):
        s_d = jnp.where(causal, _dot_nt(qs[h], k_blk(h, start)), -jnp.inf)
        s_p = jnp.where(real_pre, _dot_nt(qs[h], kp_ref[:, hq(h)]), -jnp.inf)
        parts = _lane_chunks(s_d) + [s_p]
        m = _row_max(parts)
        p_parts = [jnp.exp2((x - m) * c_exp) for x in parts]
        m_ref[:, hl(h)] = m
        acc_ref[:, hq(h)] = (_dot(jnp.concatenate(p_parts[:-1], axis=1).astype(BF16), v_blk(h, start))
                             + _dot(p_parts[-1].astype(BF16), vp_ref[:, hq(h)]))

    def body(jb, carry):
        start = pl.multiple_of(jb * tk, tk)
        for h in range(n_grp):
            parts = _lane_chunks(_dot_nt(qs[h], k_blk(h, start)))
            m_old = m_ref[:, hl(h)]
            m_new = jnp.maximum(m_old, _row_max(parts))
            a = jnp.exp2((m_old - m_new) * c_exp)
            p = jnp.concatenate([jnp.exp2((x - m_new) * c_exp) for x in parts], axis=1).astype(BF16)
            acc_ref[:, hq(h)] = jnp.concatenate([a, a], axis=1) * acc_ref[:, hq(h)] + _dot(p, v_blk(h, start))
            m_ref[:, hl(h)] = m_new
        return carry

    lax.fori_loop(0, qi, body, 0)
    for h in range(n_grp):
        acc = acc_ref[:, hq(h)]
        o_ref[:, hl(h)] = (acc[:, :LANES] / acc[:, LANES:]).astype(o_ref.dtype)


def _attention(q, k, v, k_pre, v_pre, n_pre, n_seq, seq_len, n_heads, scale, t_pref=512, heads_per_step=4):
    dq = 2 * LANES
    t = _tile(seq_len, t_pref)
    n_t = seq_len // t
    g = max(x for x in range(1, heads_per_step + 1) if n_heads % x == 0)
    assert t % LANES == 0 and k_pre.shape[0] == LANES and v.shape[1] == n_heads * dq
    return pl.pallas_call(
        functools.partial(_attn_kernel, c_exp=scale * LOG2_E, tk=t, n_pre=n_pre),
        grid=(n_seq, n_heads // g, n_t),
        in_specs=[
            pl.BlockSpec((t, g * dq), lambda b, h, i: (b * n_t + i, h)),
            pl.BlockSpec((seq_len, g * dq), lambda b, h, i: (b, h)),
            pl.BlockSpec((seq_len, g * dq), lambda b, h, i: (b, h)),
            pl.BlockSpec((LANES, g * dq), lambda b, h, i: (0, h)),
            pl.BlockSpec((LANES, g * dq), lambda b, h, i: (0, h)),
        ],
        out_specs=pl.BlockSpec((t, g * LANES), lambda b, h, i: (b * n_t + i, h)),
        out_shape=jax.ShapeDtypeStruct((n_seq * seq_len, n_heads * LANES), BF16),
        scratch_shapes=[pltpu.VMEM((t, g * LANES), F32), pltpu.VMEM((t, g * dq), F32)],
        compiler_params=_params("parallel", "parallel", "arbitrary"),
        name="attn_causal",
    )(q, k, v, k_pre, v_pre)


def _prefix_attn_kernel(q_ref, k_ref, v_ref, o_ref, *, scale):
    s = _dot_nt(q_ref[...], k_ref[...]) * scale
    row = lax.broadcasted_iota(jnp.int32, s.shape, 0)
    col = lax.broadcasted_iota(jnp.int32, s.shape, 1)
    s = jnp.where(col <= row, s, -jnp.inf)
    p = jnp.exp(s - jnp.max(s, axis=1, keepdims=True))
    o = _dot(p.astype(BF16), v_ref[:, :LANES]) / jnp.sum(p, axis=1, keepdims=True)
    o_ref[...] = o.astype(o_ref.dtype)


def _prefix_attention(q, k, v, n_heads, scale):
    n = q.shape[0]
    dq = 2 * LANES
    blk = pl.BlockSpec((n, dq), lambda h: (0, h))
    return pl.pallas_call(
        functools.partial(_prefix_attn_kernel, scale=scale),
        grid=(n_heads,),
        in_specs=[blk, blk, blk],
        out_specs=pl.BlockSpec((n, LANES), lambda h: (0, h)),
        out_shape=jax.ShapeDtypeStruct((n, n_heads * LANES), BF16),
        compiler_params=_params("parallel"),
        name="attn_prefix",
    )(q, k, v)


def _absorb_q_kernel(q_ref, wuk_ref, o_ref):
    o_ref[...] = _dot_nt(q_ref[:, :LANES], wuk_ref[...]).astype(o_ref.dtype)


def _absorb_q(q_cat, wuk, j, n_rows, n_heads):
    kv_lora = wuk.shape[1]
    return pl.pallas_call(
        _absorb_q_kernel,
        grid=(n_heads,),
        in_specs=[pl.BlockSpec((n_rows, 2 * LANES), lambda h: (0, h)),
                  pl.BlockSpec((None, kv_lora, LANES), lambda h: (j, 0, h))],
        out_specs=pl.BlockSpec((None, n_rows, kv_lora), lambda h: (h, 0, 0)),
        out_shape=jax.ShapeDtypeStruct((n_heads, n_rows, kv_lora), BF16),
        compiler_params=_params("parallel"),
        name="absorb_q",
    )(q_cat, wuk)


def _expand_o_kernel(o_ref, wuv_ref, out_ref):
    out_ref[...] = _dot(o_ref[...], wuv_ref[...]).astype(out_ref.dtype)


def _expand_o(o_lat, wuv, j, n_heads):
    _, n_rows, kv_lora = o_lat.shape
    hv = wuv.shape[-1] // n_heads
    return pl.pallas_call(
        _expand_o_kernel,
        grid=(n_heads,),
        in_specs=[pl.BlockSpec((None, n_rows, kv_lora), lambda h: (h, 0, 0)),
                  pl.BlockSpec((None, kv_lora, hv), lambda h: (j, 0, h))],
        out_specs=pl.BlockSpec((n_rows, hv), lambda h: (0, h)),
        out_shape=jax.ShapeDtypeStruct((n_rows, n_heads * hv), BF16),
        compiler_params=_params("parallel"),
        name="expand_o",
    )(o_lat, wuv)


def _sample_attn_kernel(pt_ref, ql_ref, qr_ref, cnew_ref, krnew_ref, *refs, scale, group):
    del pt_ref
    ckv_refs, krt_refs = refs[:group], refs[group:2 * group]
    o_ref, m_ref, l_ref, acc_ref = refs[2 * group:]
    c = pl.program_id(1)
    ql, qr = ql_ref[...], qr_ref[...]

    @pl.when(c == 0)
    def _():
        cn = cnew_ref[...].astype(BF16).astype(F32)
        kn = krnew_ref[...].astype(BF16).astype(F32)
        s_new = (jnp.sum(ql.astype(F32) * cn, axis=-1, keepdims=True)
                 + jnp.sum(qr.astype(F32) * kn, axis=-1, keepdims=True)) * scale
        m_ref[...] = s_new
        l_ref[...] = jnp.ones_like(l_ref)
        acc_ref[...] = jnp.broadcast_to(cn, acc_ref.shape)

    pages = [r[...].astype(BF16) for r in ckv_refs]
    s = jnp.concatenate(
        [_dot_nt(ql, pages[g]) + _dot(qr, krt_refs[g][...].astype(BF16)) for g in range(group)],
        axis=1) * scale
    m_old = m_ref[...]
    m_new = jnp.maximum(m_old, jnp.max(s, axis=-1, keepdims=True))
    a = jnp.exp(m_old - m_new)
    p = jnp.exp(s - m_new)
    l_ref[...] = a * l_ref[...] + jnp.sum(p, axis=-1, keepdims=True)
    pb = p.astype(BF16)
    page = pages[0].shape[0]
    pv = _dot(pb[:, :page], pages[0])
    for g in range(1, group):
        pv += _dot(pb[:, g * page:(g + 1) * page], pages[g])
    acc_ref[...] = a * acc_ref[...] + pv
    m_ref[...] = m_new

    @pl.when(c == pl.num_programs(1) - 1)
    def _():
        o_ref[...] = (acc_ref[...] / l_ref[...]).astype(o_ref.dtype)


def _sample_attention(q_lat, q_rope, c_new, kr_new, cache_ckv, cache_krt, page_table, j, scale, group_pref=64):
    n, n_heads, kv_lora = q_lat.shape
    rope_dim = q_rope.shape[-1]
    n_pages, page = page_table.shape[1], cache_ckv.shape[2]
    group = max(g for g in range(1, group_pref + 1) if n_pages % g == 0)
    pt_flat = page_table.reshape(-1)

    def page_spec(shape, g):
        return pl.BlockSpec((None, None) + shape, lambda b, c, pt: (j, pt[b * n_pages + c * group + g], 0, 0))

    per_row = lambda s1, w: pl.BlockSpec((None, s1, w), lambda b, c, pt: (b, 0, 0))
    grid_spec = pltpu.PrefetchScalarGridSpec(
        num_scalar_prefetch=1,
        grid=(n, n_pages // group),
        in_specs=[per_row(n_heads, kv_lora), per_row(n_heads, rope_dim), per_row(1, kv_lora), per_row(1, rope_dim)]
        + [page_spec((page, kv_lora), g) for g in range(group)]
        + [page_spec((rope_dim, page), g) for g in range(group)],
        out_specs=per_row(n_heads, kv_lora),
        scratch_shapes=[pltpu.VMEM((n_heads, 1), F32), pltpu.VMEM((n_heads, 1), F32),
                        pltpu.VMEM((n_heads, kv_lora), F32)],
    )
    return pl.pallas_call(
        functools.partial(_sample_attn_kernel, scale=scale, group=group),
        grid_spec=grid_spec,
        out_shape=jax.ShapeDtypeStruct((n, n_heads, kv_lora), BF16),
        compiler_params=_params("parallel", "arbitrary"),
        name="sample_attn",
    )(pt_flat, q_lat, q_rope, c_new, kr_new, *([cache_ckv] * group), *([cache_krt] * group))


def _proj_ln_kernel(a_ref, x_ref, w_ref, g_ref, b_ref, o_ref, *, alpha):
    y = alpha * x_ref[...] + _dot(a_ref[...], w_ref[...])
    o_ref[...] = _layer_norm(y, g_ref[...], b_ref[...])


def _proj_ln(a, x, w, g, b, j, layer, alpha, tm_pref=512):
    m, d = x.shape
    k = a.shape[1]
    tm = _tile(m, tm_pref)
    return pl.pallas_call(
        functools.partial(_proj_ln_kernel, alpha=alpha),
        grid=(m // tm,),
        in_specs=[
            pl.BlockSpec((tm, k), lambda i: (i, 0)),
            pl.BlockSpec((tm, d), lambda i: (i, 0)),
            _resident((None, k, d), lambda i: (j, 0, 0)),
            pl.BlockSpec((None, None, 1, d), lambda i: (layer, 1, 0, 0)),
            pl.BlockSpec((None, None, 1, d), lambda i: (layer, 1, 0, 0)),
        ],
        out_specs=pl.BlockSpec((tm, d), lambda i: (i, 0)),
        out_shape=jax.ShapeDtypeStruct((m, d), F32),
        compiler_params=_params("parallel"),
        name="proj_ln",
    )(a, x, w, g, b)


def _rope_tables(pos, rope_dim):
    inv = ROPE_THETA ** (-jnp.arange(0, rope_dim, 2, dtype=F32) / rope_dim)
    ang = pos.astype(F32)[:, None] * inv[None, :]
    cos, sin = jnp.cos(ang), jnp.sin(ang)
    reps = LANES // rope_dim
    return jnp.tile(jnp.concatenate([cos, cos], -1), (1, reps)), jnp.tile(jnp.concatenate([-sin, sin], -1), (1, reps))


def _swap_halves(w):
    half = w.shape[-1] // 2
    return jnp.concatenate([w[..., half:], w[..., :half]], axis=-1)


def _pad_rows(a, rows):
    return jnp.pad(a, ((0, rows - a.shape[0]), (0, 0)))


def kernel(x_prompt, x_sample, state_conv, cache_kv_latent, cache_k_rope, page_table, meta_tokens, ln_g, ln_b, w_ffn_gate, w_ffn_up, w_ffn_down, w_conv_in, w_conv, w_conv_out, w_q_a, g_q_a, w_q_b, w_kv_a, g_kv_a, w_uk, w_uv, w_o):
    nb, seq, d = x_prompt.shape
    ns, dec_seq, _ = x_sample.shape
    assert dec_seq == 1, "the sample path handles one new token per sequence"
    n_meta = meta_tokens.shape[0]
    depth = ln_g.shape[0]
    alpha = float((2 * depth) ** 0.25)
    n_heads = w_q_b.shape[2]
    rope_dim = cache_k_rope.shape[-1]
    nope_dim = w_q_b.shape[3] - rope_dim
    kv_lora = w_uk.shape[1]
    scale = float((nope_dim + rope_dim) ** -0.5)
    past = page_table.shape[1] * cache_kv_latent.shape[2]
    assert nope_dim == LANES and w_uv.shape[-1] == LANES and n_heads % 2 == 0 and n_meta <= LANES

    xm = x_prompt.reshape(nb * seq, d)
    xe = jnp.concatenate([x_sample.reshape(ns, d), meta_tokens.astype(x_prompt.dtype)], axis=0)
    n_ext = ns + n_meta

    wqa = w_q_a.astype(BF16)
    wqn = w_q_b[..., :nope_dim].reshape(w_q_b.shape[0], w_q_b.shape[1], n_heads * nope_dim).astype(BF16)
    wq_rope = w_q_b[..., nope_dim:]
    wqr = wq_rope.reshape(w_q_b.shape[0], w_q_b.shape[1], n_heads * rope_dim).astype(BF16)
    wqrs = _swap_halves(wq_rope).reshape(wqr.shape).astype(BF16)
    wk_rope = w_kv_a[..., kv_lora:]
    zpad = jnp.zeros(wk_rope.shape[:-1] + (LANES - rope_dim,), w_kv_a.dtype)
    wkv = jnp.concatenate([w_kv_a[..., :kv_lora], wk_rope, zpad, _swap_halves(wk_rope), zpad], axis=-1).astype(BF16)
    wuk = w_uk.reshape(w_uk.shape[0], kv_lora, n_heads * nope_dim).astype(BF16)
    wuv = w_uv.reshape(w_uv.shape[0], kv_lora, -1).astype(BF16)
    wo = w_o.astype(BF16)
    ln_g4, ln_b4 = ln_g[:, :, None, :], ln_b[:, :, None, :]
    g_q3, g_kv3 = g_q_a[:, None, :], g_kv_a[:, None, :]
    cache_krt = jnp.swapaxes(cache_k_rope, 2, 3)

    cos_m, sin_m = _rope_tables(jnp.tile(n_meta + jnp.arange(seq), nb), rope_dim)
    cos_e, sin_e = _rope_tables(jnp.concatenate([jnp.full((ns,), past), jnp.arange(n_meta)]), rope_dim)

    def ffn_both(xm, xe, layer, half, ln_idx):
        xe, wg, wu, wd = _ffn_cast(xe, w_ffn_gate, w_ffn_up, w_ffn_down, ln_g4, ln_b4, layer, half, ln_idx, alpha)
        return _ffn(xm, wg, wu, wd, ln_g4, ln_b4, layer, ln_idx, alpha), xe

    conv_p, conv_s, ckv_p, ckv_s, kr_p, kr_s = [], [], [], [], [], []
    for i in range(depth):
        j = i // N_MIXERS
        xm, xe = ffn_both(xm, xe, i, 0, 0)
        if i % N_MIXERS == 0:
            dc = w_conv_out.shape[1]
            st = state_conv[j]
            zeros = jnp.zeros((n_meta, dc), F32)
            p1 = jnp.concatenate([st[:, 1], zeros], axis=0)
            p2 = jnp.concatenate([st[:, 0], zeros], axis=0)
            rows = jnp.arange(n_ext)[:, None]
            f1 = (rows <= ns).astype(F32)
            f2 = (rows <= ns + 1).astype(F32)
            xe, u_e, wb, wc, wh, wout = _conv_mixer_rows(xe, p1, p2, f1, f2, w_conv_in, w_conv, w_conv_out,
                                                         ln_g4, ln_b4, j, i, alpha)
            prev = jnp.broadcast_to(jnp.pad(u_e[n_ext - 2:], ((SUBLANES - 2, 0), (0, 0)))[None], (nb, SUBLANES, dc))
            xm, u_last = _conv_mixer_seq(xm, prev, wb, wc, wh, w_conv, wout, ln_g4, ln_b4, j, i, alpha, seq)
            tiles_per_seq = u_last.shape[0] // nb
            conv_p.append(u_last[tiles_per_seq - 1::tiles_per_seq, SUBLANES - 2:])
            conv_s.append(jnp.stack([st[:, 1], u_e[:ns]], axis=1))
        else:
            proj = functools.partial(_mla_proj, wqa=wqa, gqa=g_q3, wqn=wqn, wqr=wqr, wqrs=wqrs, wkv=wkv, gkv=g_kv3,
                                     wuk=wuk, wuv=wuv, j=j, n_heads=n_heads)
            qc_e, kc_e, v_e, ckv_e, kr_e = proj(xe, cos_e, sin_e)
            qc_m, kc_m, v_m, ckv_m, kr_m = proj(xm, cos_m, sin_m)
            o_meta = _prefix_attention(qc_e[ns:], kc_e[ns:], v_e[ns:], n_heads, scale)
            o_main = _attention(qc_m, kc_m, v_m, _pad_rows(kc_e[ns:], LANES), _pad_rows(v_e[ns:], LANES), n_meta,
                                nb, seq, n_heads, scale)
            q_lat = _absorb_q(qc_e, wuk, j, ns, n_heads).transpose(1, 0, 2)
            q_rope_s = qc_e[:ns].reshape(ns, n_heads, 2 * LANES)[:, :, LANES:LANES + rope_dim]
            o_lat = _sample_attention(q_lat, q_rope_s, ckv_e[:ns, None], kr_e[:ns, None],
                                      cache_kv_latent, cache_krt, page_table, j, scale)
            o_s = _expand_o(o_lat.transpose(1, 0, 2), wuv, j, n_heads)
            o_e = jnp.concatenate([o_s, o_meta], axis=0)
            xe = _proj_ln(o_e, xe, wo, ln_g4, ln_b4, j, i, alpha)
            xm = _proj_ln(o_main, xm, wo, ln_g4, ln_b4, j, i, alpha)
            meta_rows = lambda a: jnp.broadcast_to(a[ns:][None], (nb, n_meta, a.shape[-1]))
            ckv_p.append(jnp.concatenate([meta_rows(ckv_e), ckv_m.reshape(nb, seq, kv_lora)], axis=1))
            kr_p.append(jnp.concatenate([meta_rows(kr_e), kr_m.reshape(nb, seq, rope_dim)], axis=1))
            ckv_s.append(ckv_e[:ns, None])
            kr_s.append(kr_e[:ns, None])
        xm, xe = ffn_both(xm, xe, i, 1, 2)

    y_prompt = xm.reshape(nb, seq, d)
    y_sample = xe[:ns].reshape(ns, dec_seq, d)
    return (y_prompt, y_sample, jnp.stack(conv_p), jnp.stack(conv_s), jnp.stack(ckv_p), jnp.stack(ckv_s),
            jnp.stack(kr_p), jnp.stack(kr_s))
```

```python
import functools

import jax
import jax.numpy as jnp
from jax import lax
from jax.experimental import pallas as pl
from jax.experimental.pallas import tpu as pltpu

LN_EPS = 1e-5
RMS_EPS = 1e-6
ROPE_THETA = 10000.0
N_MIXERS = 2
LOG2_E = 1.4426950408889634
FFN_SUB = 256

LANES = 128
SUBLANES = 8
VMEM_LIMIT_BYTES = 56 * 1024 * 1024

BF16 = jnp.bfloat16
F32 = jnp.float32


def _tile(n, pref):
    if n <= pref:
        return n
    for t in range(pref, 15, -1):
        if n % t == 0 and t % 16 == 0:
            return t
    return n


def _params(*sem):
    return pltpu.CompilerParams(dimension_semantics=sem, vmem_limit_bytes=VMEM_LIMIT_BYTES)


def _resident(shape, index_map):
    return pl.BlockSpec(shape, index_map, pipeline_mode=pl.Buffered(1))


def _layer_norm(y, g, b):
    mu = jnp.mean(y, axis=-1, keepdims=True)
    yc = y - mu
    var = jnp.mean(yc * yc, axis=-1, keepdims=True)
    return yc * lax.rsqrt(var + LN_EPS) * g + b


def _rms_norm(y, g):
    return y * lax.rsqrt(jnp.mean(y * y, axis=-1, keepdims=True) + RMS_EPS) * g


def _dot(a, b):
    return jnp.dot(a, b, preferred_element_type=F32)


def _dot_nt(a, b):
    return lax.dot_general(a, b, (((1,), (1,)), ((), ())), preferred_element_type=F32)


def _ffn_step(f, n_f, x_ref, wg_ref, wu_ref, wd_ref, g_ref, b_ref, o_ref, xb_ref, acc_ref, alpha):
    @pl.when(f == 0)
    def _():
        xb_ref[...] = x_ref[...].astype(BF16)
        acc_ref[...] = jnp.zeros_like(acc_ref)

    xb = xb_ref[...]
    tf = wg_ref.shape[1]
    sub = FFN_SUB if tf % FFN_SUB == 0 else tf
    part = None
    for c in range(tf // sub):
        cols = slice(c * sub, (c + 1) * sub)
        gate = _dot(xb, wg_ref[:, cols])
        up = _dot(xb, wu_ref[:, cols])
        h = gate / (1.0 + jnp.exp(-gate)) * up
        d = _dot(h.astype(BF16), wd_ref[cols, :])
        part = d if part is None else part + d
    acc_ref[...] += part

    @pl.when(f == n_f - 1)
    def _():
        y = alpha * x_ref[...] + 0.5 * acc_ref[...]
        o_ref[...] = _layer_norm(y, g_ref[...], b_ref[...])


def _ffn_kernel(x_ref, wg_ref, wu_ref, wd_ref, g_ref, b_ref, o_ref, xb_ref, acc_ref, *, alpha):
    _ffn_step(pl.program_id(1), pl.num_programs(1), x_ref, wg_ref, wu_ref, wd_ref,
              g_ref, b_ref, o_ref, xb_ref, acc_ref, alpha)


def _ffn_cast_kernel(x_ref, wg_ref, wu_ref, wd_ref, g_ref, b_ref, o_ref, wgb_ref, wub_ref, wdb_ref,
                     xb_ref, acc_ref, *, alpha):
    wgb_ref[...] = wg_ref[...].astype(BF16)
    wub_ref[...] = wu_ref[...].astype(BF16)
    wdb_ref[...] = wd_ref[...].astype(BF16)
    _ffn_step(pl.program_id(0), pl.num_programs(0), x_ref, wgb_ref, wub_ref, wdb_ref,
              g_ref, b_ref, o_ref, xb_ref, acc_ref, alpha)


def _ffn_cast_next_kernel(x_ref, wg_ref, wu_ref, wd_ref, g_ref, b_ref, ng_ref, nu_ref, nd_ref,
                          o_ref, ngb_ref, nub_ref, ndb_ref, xb_ref, acc_ref, *, alpha):
    ngb_ref[...] = ng_ref[...].astype(BF16)
    nub_ref[...] = nu_ref[...].astype(BF16)
    ndb_ref[...] = nd_ref[...].astype(BF16)
    _ffn_step(pl.program_id(1), pl.num_programs(1), x_ref, wg_ref, wu_ref, wd_ref,
              g_ref, b_ref, o_ref, xb_ref, acc_ref, alpha)


def _ffn_tiles(m, d, ff, tm_pref=512, tf_pref=512):
    return _tile(m, tm_pref), _tile(ff, tf_pref)


def _can_cast_next(m, d, ff):
    tm, _ = _ffn_tiles(m, d, ff)
    n_i = m // tm
    return d % n_i == 0 and (d // n_i) % LANES == 0


def _ffn(x, wg, wu, wd, g, b, layer, ln_idx, alpha, cast_next=None):
    m, d = x.shape
    ff = wg.shape[-1]
    tm, tf = _ffn_tiles(m, d, ff)
    n_i = m // tm
    in_specs = [
        pl.BlockSpec((tm, d), lambda i, f: (i, 0)),
        pl.BlockSpec((d, tf), lambda i, f: (0, f)),
        pl.BlockSpec((d, tf), lambda i, f: (0, f)),
        pl.BlockSpec((tf, d), lambda i, f: (f, 0)),
        pl.BlockSpec((None, None, 1, d), lambda i, f: (layer, ln_idx, 0, 0)),
        pl.BlockSpec((None, None, 1, d), lambda i, f: (layer, ln_idx, 0, 0)),
    ]
    out_spec = pl.BlockSpec((tm, d), lambda i, f: (i, 0))
    out_shape = jax.ShapeDtypeStruct((m, d), F32)
    scratch = [pltpu.VMEM((tm, d), BF16), pltpu.VMEM((tm, d), F32)]
    if cast_next is None:
        return pl.pallas_call(
            functools.partial(_ffn_kernel, alpha=alpha),
            grid=(n_i, ff // tf),
            in_specs=in_specs,
            out_specs=out_spec,
            out_shape=out_shape,
            scratch_shapes=scratch,
            compiler_params=_params("parallel", "arbitrary"),
            name="ffn_macaron",
        )(x, wg, wu, wd, g, b)
    ng, nu, nd, layer_n, half_n = cast_next
    td = d // n_i
    return pl.pallas_call(
        functools.partial(_ffn_cast_next_kernel, alpha=alpha),
        grid=(n_i, ff // tf),
        in_specs=in_specs + [
            pl.BlockSpec((None, None, td, tf), lambda i, f: (layer_n, half_n, i, f)),
            pl.BlockSpec((None, None, td, tf), lambda i, f: (layer_n, half_n, i, f)),
            pl.BlockSpec((None, None, tf, td), lambda i, f: (layer_n, half_n, f, i)),
        ],
        out_specs=[out_spec,
                   pl.BlockSpec((td, tf), lambda i, f: (i, f)),
                   pl.BlockSpec((td, tf), lambda i, f: (i, f)),
                   pl.BlockSpec((tf, td), lambda i, f: (f, i))],
        out_shape=[out_shape, jax.ShapeDtypeStruct((d, ff), BF16), jax.ShapeDtypeStruct((d, ff), BF16),
                   jax.ShapeDtypeStruct((ff, d), BF16)],
        scratch_shapes=scratch,
        compiler_params=_params("parallel", "arbitrary"),
        name="ffn_macaron_cast_next",
    )(x, wg, wu, wd, g, b, ng, nu, nd)


def _ffn_cast(x, wg, wu, wd, g, b, layer, half, ln_idx, alpha, tf_pref=256):
    m, d = x.shape
    ff = wg.shape[-1]
    tf = _tile(ff, tf_pref)
    return pl.pallas_call(
        functools.partial(_ffn_cast_kernel, alpha=alpha),
        grid=(ff // tf,),
        in_specs=[
            pl.BlockSpec((m, d), lambda f: (0, 0)),
            pl.BlockSpec((None, None, d, tf), lambda f: (layer, half, 0, f)),
            pl.BlockSpec((None, None, d, tf), lambda f: (layer, half, 0, f)),
            pl.BlockSpec((None, None, tf, d), lambda f: (layer, half, f, 0)),
            pl.BlockSpec((None, None, 1, d), lambda f: (layer, ln_idx, 0, 0)),
            pl.BlockSpec((None, None, 1, d), lambda f: (layer, ln_idx, 0, 0)),
        ],
        out_specs=[
            pl.BlockSpec((m, d), lambda f: (0, 0)),
            pl.BlockSpec((d, tf), lambda f: (0, f)),
            pl.BlockSpec((d, tf), lambda f: (0, f)),
            pl.BlockSpec((tf, d), lambda f: (f, 0)),
        ],
        out_shape=[jax.ShapeDtypeStruct((m, d), F32), jax.ShapeDtypeStruct((d, ff), BF16),
                   jax.ShapeDtypeStruct((d, ff), BF16), jax.ShapeDtypeStruct((ff, d), BF16)],
        scratch_shapes=[pltpu.VMEM((m, d), BF16), pltpu.VMEM((m, d), F32)],
        compiler_params=_params("arbitrary"),
        name="ffn_macaron_cast",
    )(x, wg, wu, wd, g, b)


def _conv_taps(u, u_m1, u_m2, wconv_ref):
    w = wconv_ref[...]
    return w[0:1] * u_m2 + w[1:2] * u_m1 + w[2:3] * u


def _conv_seq_kernel(x_ref, prev_ref, wb_ref, wc_ref, wh_ref, wconv_ref, wout_ref, g_ref, b_ref,
                     o_ref, ulast_ref, xb_ref, acc_ref, carry_ref, *, alpha, tiles_per_seq):
    i, c = pl.program_id(0), pl.program_id(1)
    tm = x_ref.shape[0]

    @pl.when(c == 0)
    def _():
        xb_ref[...] = x_ref[...].astype(BF16)
        acc_ref[...] = jnp.zeros_like(acc_ref)

    @pl.when(i == 0)
    def _():
        carry_ref[c] = jnp.zeros(carry_ref.shape[1:], F32)

    xb = xb_ref[...]
    u = _dot(xb, wc_ref[...]) * _dot(xb, wh_ref[...])
    seq_start = (i % tiles_per_seq) == 0
    tail = jnp.where(seq_start, prev_ref[...], carry_ref[c])
    row = lax.broadcasted_iota(jnp.int32, u.shape, 0)
    u_m1 = jnp.where(row == 0, tail[7:8], pltpu.roll(u, 1, 0))
    u_m2 = jnp.where(row == 0, tail[6:7], jnp.where(row == 1, tail[7:8], pltpu.roll(u, 2, 0)))
    conv = _conv_taps(u, u_m1, u_m2, wconv_ref)
    last = u[tm - SUBLANES:, :]
    carry_ref[c] = last
    ulast_ref[...] = last
    z = _dot(xb, wb_ref[...]) * conv
    acc_ref[...] += _dot(z.astype(BF16), wout_ref[...])

    @pl.when(c == pl.num_programs(1) - 1)
    def _():
        o_ref[...] = _layer_norm(alpha * x_ref[...] + acc_ref[...], g_ref[...], b_ref[...])


def _conv_rows_kernel(x_ref, p1_ref, p2_ref, f1_ref, f2_ref, wb_ref, wc_ref, wh_ref, wconv_ref, wout_ref,
                      g_ref, b_ref, o_ref, u_ref, wbb_ref, wcb_ref, whb_ref, woutb_ref, xb_ref, acc_ref, *, alpha):
    c = pl.program_id(0)

    @pl.when(c == 0)
    def _():
        xb_ref[...] = x_ref[...].astype(BF16)
        acc_ref[...] = jnp.zeros_like(acc_ref)

    wb, wc, wh = wb_ref[...].astype(BF16), wc_ref[...].astype(BF16), wh_ref[...].astype(BF16)
    wout = wout_ref[...].astype(BF16)
    wbb_ref[...] = wb
    wcb_ref[...] = wc
    whb_ref[...] = wh
    woutb_ref[...] = wout

    xb = xb_ref[...]
    u = _dot(xb, wc) * _dot(xb, wh)
    u_m1 = jnp.where(f1_ref[...] > 0.5, p1_ref[...], pltpu.roll(u, 1, 0))
    u_m2 = jnp.where(f2_ref[...] > 0.5, p2_ref[...], pltpu.roll(u, 2, 0))
    conv = _conv_taps(u, u_m1, u_m2, wconv_ref)
    u_ref[...] = u
    z = _dot(xb, wb) * conv
    acc_ref[...] += _dot(z.astype(BF16), wout)

    @pl.when(c == pl.num_programs(0) - 1)
    def _():
        o_ref[...] = _layer_norm(alpha * x_ref[...] + acc_ref[...], g_ref[...], b_ref[...])


def _conv_mixer_seq(x, prev, wb, wc, wh, w_conv, w_out, g, b, j, layer, alpha, seq_len, tm_pref=512, tn_pref=512):
    m, d = x.shape
    dc = w_out.shape[0]
    tm, tn = _tile(seq_len, tm_pref), _tile(dc, tn_pref)
    tiles_per_seq = seq_len // tm
    n_c = dc // tn
    w_spec = pl.BlockSpec((d, tn), lambda i, c: (0, c))
    return pl.pallas_call(
        functools.partial(_conv_seq_kernel, alpha=alpha, tiles_per_seq=tiles_per_seq),
        grid=(m // tm, n_c),
        in_specs=[
            pl.BlockSpec((tm, d), lambda i, c: (i, 0)),
            pl.BlockSpec((None, SUBLANES, tn), lambda i, c: (i // tiles_per_seq, 0, c)),
            w_spec, w_spec, w_spec,
            pl.BlockSpec((None, 3, tn), lambda i, c: (j, 0, c)),
            pl.BlockSpec((tn, d), lambda i, c: (c, 0)),
            pl.BlockSpec((None, None, 1, d), lambda i, c: (layer, 1, 0, 0)),
            pl.BlockSpec((None, None, 1, d), lambda i, c: (layer, 1, 0, 0)),
        ],
        out_specs=[
            pl.BlockSpec((tm, d), lambda i, c: (i, 0)),
            pl.BlockSpec((None, SUBLANES, tn), lambda i, c: (i, 0, c)),
        ],
        out_shape=[jax.ShapeDtypeStruct((m, d), F32),
                   jax.ShapeDtypeStruct((m // tm, SUBLANES, dc), F32)],
        scratch_shapes=[pltpu.VMEM((tm, d), BF16), pltpu.VMEM((tm, d), F32),
                        pltpu.VMEM((n_c, SUBLANES, tn), F32)],
        compiler_params=_params("arbitrary", "arbitrary"),
        name="conv_mixer_seq",
    )(x, prev, wb, wc, wh, w_conv, w_out, g, b)


def _conv_mixer_rows(x, p1, p2, f1, f2, w_in, w_conv, w_out, g, b, j, layer, alpha, tn_pref=256):
    m, d = x.shape
    dc = w_out.shape[1]
    tn = _tile(dc, tn_pref)
    n_c = dc // tn
    w_spec = lambda part: pl.BlockSpec((None, d, tn), lambda c: (j, 0, part * n_c + c))
    wb_out = pl.BlockSpec((d, tn), lambda c: (0, c))
    return pl.pallas_call(
        functools.partial(_conv_rows_kernel, alpha=alpha),
        grid=(n_c,),
        in_specs=[
            pl.BlockSpec((m, d), lambda c: (0, 0)),
            pl.BlockSpec((m, tn), lambda c: (0, c)),
            pl.BlockSpec((m, tn), lambda c: (0, c)),
            pl.BlockSpec((m, 1), lambda c: (0, 0)),
            pl.BlockSpec((m, 1), lambda c: (0, 0)),
            w_spec(0), w_spec(1), w_spec(2),
            pl.BlockSpec((None, 3, tn), lambda c: (j, 0, c)),
            pl.BlockSpec((None, tn, d), lambda c: (j, c, 0)),
            pl.BlockSpec((None, None, 1, d), lambda c: (layer, 1, 0, 0)),
            pl.BlockSpec((None, None, 1, d), lambda c: (layer, 1, 0, 0)),
        ],
        out_specs=[
            pl.BlockSpec((m, d), lambda c: (0, 0)),
            pl.BlockSpec((m, tn), lambda c: (0, c)),
            wb_out, wb_out, wb_out,
            pl.BlockSpec((tn, d), lambda c: (c, 0)),
        ],
        out_shape=[jax.ShapeDtypeStruct((m, d), F32), jax.ShapeDtypeStruct((m, dc), F32),
                   jax.ShapeDtypeStruct((d, dc), BF16), jax.ShapeDtypeStruct((d, dc), BF16),
                   jax.ShapeDtypeStruct((d, dc), BF16), jax.ShapeDtypeStruct((dc, d), BF16)],
        scratch_shapes=[pltpu.VMEM((m, d), BF16), pltpu.VMEM((m, d), F32)],
        compiler_params=_params("arbitrary"),
        name="conv_mixer_rows",
    )(x, p1, p2, f1, f2, w_in, w_in, w_in, w_conv, w_out, g, b)


def _mla_proj_kernel(x_ref, cos_ref, sin_ref, wqa_ref, gqa_ref, wqn_ref, wqr_ref, wqrs_ref, wkv_ref, gkv_ref,
                     wuk_ref, wuv_ref, qc_ref, kc_ref, v_ref, ckv_ref, kr_ref, *, n_heads, kv_lora, rope_dim):
    half = LANES // 2
    assert rope_dim == half, "rope halves are packed two heads per 128 lanes"
    xb = x_ref[...].astype(BF16)
    cos, sin = cos_ref[...], sin_ref[...]
    lane = lax.broadcasted_iota(jnp.int32, cos.shape, 1)
    low = lane < half

    cq = _rms_norm(_dot(xb, wqa_ref[...]), gqa_ref[...]).astype(BF16)
    qn = _dot(cq, wqn_ref[...])
    qr_a = _dot(cq, wqr_ref[...])
    qr_b = _dot(cq, wqrs_ref[...])
    for pair in range(n_heads // 2):
        sl = slice(pair * LANES, (pair + 1) * LANES)
        both = qr_a[:, sl] * cos + qr_b[:, sl] * sin
        for k in range(2):
            h = 2 * pair + k
            part = both if k == 0 else pltpu.roll(both, half, 1)
            qc_ref[:, h * 2 * LANES:h * 2 * LANES + LANES] = qn[:, h * LANES:(h + 1) * LANES].astype(BF16)
            qc_ref[:, h * 2 * LANES + LANES:(h + 1) * 2 * LANES] = jnp.where(low, part, 0.0).astype(BF16)

    kv = _dot(xb, wkv_ref[...])
    ckv = _rms_norm(kv[:, :kv_lora], gkv_ref[...])
    ckv_ref[...] = ckv
    kr = kv[:, kv_lora:kv_lora + LANES] * cos + kv[:, kv_lora + LANES:] * sin
    kr = jnp.where(low, kr, 0.0)
    kr_ref[...] = kr[:, :rope_dim]
    ckv_b = ckv.astype(BF16)
    kn = _dot(ckv_b, wuk_ref[...])
    v = _dot(ckv_b, wuv_ref[...])
    kr_b = kr.astype(BF16)
    ones = jnp.ones((xb.shape[0], LANES), BF16)
    for h in range(n_heads):
        kc_ref[:, h * 2 * LANES:h * 2 * LANES + LANES] = kn[:, h * LANES:(h + 1) * LANES].astype(BF16)
        kc_ref[:, h * 2 * LANES + LANES:(h + 1) * 2 * LANES] = kr_b
        v_ref[:, h * 2 * LANES:h * 2 * LANES + LANES] = v[:, h * LANES:(h + 1) * LANES].astype(BF16)
        v_ref[:, h * 2 * LANES + LANES:(h + 1) * 2 * LANES] = ones


def _mla_proj(x, cos, sin, wqa, gqa, wqn, wqr, wqrs, wkv, gkv, wuk, wuv, j, n_heads, tm_pref=256):
    m, d = x.shape
    kv_lora = wuk.shape[1]
    rope_dim = wqr.shape[-1] // n_heads
    tm = _tile(m, tm_pref)
    hq = n_heads * 2 * LANES
    hv = hq
    row = lambda w: pl.BlockSpec((tm, w), lambda i: (i, 0))
    full = lambda a: _resident((None,) + a.shape[1:], lambda i: (j,) + (0,) * (a.ndim - 1))
    return pl.pallas_call(
        functools.partial(_mla_proj_kernel, n_heads=n_heads, kv_lora=kv_lora, rope_dim=rope_dim),
        grid=(m // tm,),
        in_specs=[row(d), row(LANES), row(LANES), full(wqa), full(gqa), full(wqn), full(wqr), full(wqrs),
                  full(wkv), full(gkv), full(wuk), full(wuv)],
        out_specs=[row(hq), row(hq), row(hv), row(kv_lora), row(rope_dim)],
        out_shape=[jax.ShapeDtypeStruct((m, hq), BF16), jax.ShapeDtypeStruct((m, hq), BF16),
                   jax.ShapeDtypeStruct((m, hv), BF16), jax.ShapeDtypeStruct((m, kv_lora), F32),
                   jax.ShapeDtypeStruct((m, rope_dim), F32)],
        compiler_params=_params("parallel"),
        name="mla_proj",
    )(x, cos, sin, wqa, gqa, wqn, wqr, wqrs, wkv, gkv, wuk, wuv)


def _lane_chunks(s):
    return [s[:, c * LANES:(c + 1) * LANES] for c in range(s.shape[1] // LANES)]


def _row_max(parts):
    m = functools.reduce(jnp.maximum, parts)
    return jnp.broadcast_to(jnp.max(m, axis=1, keepdims=True), m.shape)


def _attn_kernel(q_ref, k_ref, v_ref, kp_ref, vp_ref, o_ref, m_ref, acc_ref, *, c_exp, tk, n_pre):
    qi = pl.program_id(2)
    dq = 2 * LANES
    n_grp = q_ref.shape[1] // dq
    assert q_ref.shape[0] == tk and v_ref.shape[1] == n_grp * dq
    qs = [q_ref[:, h * dq:(h + 1) * dq] for h in range(n_grp)]
    hq = lambda h: slice(h * dq, (h + 1) * dq)
    k_blk = lambda h, start: k_ref[pl.ds(start, tk), hq(h)]
    v_blk = lambda h, start: v_ref[pl.ds(start, tk), hq(h)]
    hl = lambda h: slice(h * LANES, (h + 1) * LANES)

    start = pl.multiple_of(qi * tk, tk)
    row = lax.broadcasted_iota(jnp.int32, (tk, tk), 0)
    col = lax.broadcasted_iota(jnp.int32, (tk, tk), 1)
    causal = col <= row
    real_pre = lax.broadcasted_iota(jnp.int32, (tk, LANES), 1) < n_pre
    for h in range(n_grp):
        s_d = jnp.where(causal, _dot_nt(qs[h], k_blk(h, start)), -jnp.inf)
        s_p = jnp.where(real_pre, _dot_nt(qs[h], kp_ref[:, hq(h)]), -jnp.inf)
        parts = _lane_chunks(s_d) + [s_p]
        m = _row_max(parts)
        p_parts = [jnp.exp2((x - m) * c_exp) for x in parts]
        m_ref[:, hl(h)] = m
        acc_ref[:, hq(h)] = (_dot(jnp.concatenate(p_parts[:-1], axis=1).astype(BF16), v_blk(h, start))
                             + _dot(p_parts[-1].astype(BF16), vp_ref[:, hq(h)]))

    def body(jb, carry):
        start = pl.multiple_of(jb * tk, tk)
        for h in range(n_grp):
            parts = _lane_chunks(_dot_nt(qs[h], k_blk(h, start)))
            m_old = m_ref[:, hl(h)]
            m_new = jnp.maximum(m_old, _row_max(parts))
            a = jnp.exp2((m_old - m_new) * c_exp)
            p = jnp.concatenate([jnp.exp2((x - m_new) * c_exp) for x in parts], axis=1).astype(BF16)
            acc_ref[:, hq(h)] = jnp.concatenate([a, a], axis=1) * acc_ref[:, hq(h)] + _dot(p, v_blk(h, start))
            m_ref[:, hl(h)] = m_new
        return carry

    lax.fori_loop(0, qi, body, 0)
    for h in range(n_grp):
        acc = acc_ref[:, hq(h)]
        o_ref[:, hl(h)] = (acc[:, :LANES] / acc[:, LANES:]).astype(o_ref.dtype)


def _attention(q, k, v, k_pre, v_pre, n_pre, n_seq, seq_len, n_heads, scale, t_pref=512, heads_per_step=4):
    dq = 2 * LANES
    t = _tile(seq_len, t_pref)
    n_t = seq_len // t
    g = max(x for x in range(1, heads_per_step + 1) if n_heads % x == 0)
    assert t % LANES == 0 and k_pre.shape[0] == LANES and v.shape[1] == n_heads * dq
    return pl.pallas_call(
        functools.partial(_attn_kernel, c_exp=scale * LOG2_E, tk=t, n_pre=n_pre),
        grid=(n_seq, n_heads // g, n_t),
        in_specs=[
            pl.BlockSpec((t, g * dq), lambda b, h, i: (b * n_t + i, h)),
            pl.BlockSpec((seq_len, g * dq), lambda b, h, i: (b, h)),
            pl.BlockSpec((seq_len, g * dq), lambda b, h, i: (b, h)),
            pl.BlockSpec((LANES, g * dq), lambda b, h, i: (0, h)),
            pl.BlockSpec((LANES, g * dq), lambda b, h, i: (0, h)),
        ],
        out_specs=pl.BlockSpec((t, g * LANES), lambda b, h, i: (b * n_t + i, h)),
        out_shape=jax.ShapeDtypeStruct((n_seq * seq_len, n_heads * LANES), BF16),
        scratch_shapes=[pltpu.VMEM((t, g * LANES), F32), pltpu.VMEM((t, g * dq), F32)],
        compiler_params=_params("parallel", "parallel", "arbitrary"),
        name="attn_causal",
    )(q, k, v, k_pre, v_pre)


def _prefix_attn_kernel(q_ref, k_ref, v_ref, o_ref, *, scale):
    s = _dot_nt(q_ref[...], k_ref[...]) * scale
    row = lax.broadcasted_iota(jnp.int32, s.shape, 0)
    col = lax.broadcasted_iota(jnp.int32, s.shape, 1)
    s = jnp.where(col <= row, s, -jnp.inf)
    p = jnp.exp(s - jnp.max(s, axis=1, keepdims=True))
    o = _dot(p.astype(BF16), v_ref[:, :LANES]) / jnp.sum(p, axis=1, keepdims=True)
    o_ref[...] = o.astype(o_ref.dtype)


def _prefix_attention(q, k, v, n_heads, scale):
    n = q.shape[0]
    dq = 2 * LANES
    blk = pl.BlockSpec((n, dq), lambda h: (0, h))
    return pl.pallas_call(
        functools.partial(_prefix_attn_kernel, scale=scale),
        grid=(n_heads,),
        in_specs=[blk, blk, blk],
        out_specs=pl.BlockSpec((n, LANES), lambda h: (0, h)),
        out_shape=jax.ShapeDtypeStruct((n, n_heads * LANES), BF16),
        compiler_params=_params("parallel"),
        name="attn_prefix",
    )(q, k, v)


def _absorb_q_kernel(q_ref, wuk_ref, o_ref):
    o_ref[...] = _dot_nt(q_ref[:, :LANES], wuk_ref[...]).astype(o_ref.dtype)


def _absorb_q(q_cat, wuk, j, n_rows, n_heads):
    kv_lora = wuk.shape[1]
    return pl.pallas_call(
        _absorb_q_kernel,
        grid=(n_heads,),
        in_specs=[pl.BlockSpec((n_rows, 2 * LANES), lambda h: (0, h)),
                  pl.BlockSpec((None, kv_lora, LANES), lambda h: (j, 0, h))],
        out_specs=pl.BlockSpec((None, n_rows, kv_lora), lambda h: (h, 0, 0)),
        out_shape=jax.ShapeDtypeStruct((n_heads, n_rows, kv_lora), BF16),
        compiler_params=_params("parallel"),
        name="absorb_q",
    )(q_cat, wuk)


def _expand_o_kernel(o_ref, wuv_ref, out_ref):
    out_ref[...] = _dot(o_ref[...], wuv_ref[...]).astype(out_ref.dtype)


def _expand_o(o_lat, wuv, j, n_heads):
    _, n_rows, kv_lora = o_lat.shape
    hv = wuv.shape[-1] // n_heads
    return pl.pallas_call(
        _expand_o_kernel,
        grid=(n_heads,),
        in_specs=[pl.BlockSpec((None, n_rows, kv_lora), lambda h: (h, 0, 0)),
                  pl.BlockSpec((None, kv_lora, hv), lambda h: (j, 0, h))],
        out_specs=pl.BlockSpec((n_rows, hv), lambda h: (0, h)),
        out_shape=jax.ShapeDtypeStruct((n_rows, n_heads * hv), BF16),
        compiler_params=_params("parallel"),
        name="expand_o",
    )(o_lat, wuv)


def _sample_attn_kernel(pt_ref, ql_ref, qr_ref, cnew_ref, krnew_ref, ckv_hbm, krt_hbm, o_ref,
                        ckv_buf, krt_buf, sem, *, scale, layer, n_pages):
    b = pl.program_id(0)
    slot = b % 2

    def page_copies(seq, slot, p):
        page_id = pt_ref[seq * n_pages + p]
        return (pltpu.make_async_copy(ckv_hbm.at[layer, page_id], ckv_buf.at[slot, p], sem.at[slot, 0]),
                pltpu.make_async_copy(krt_hbm.at[layer, page_id], krt_buf.at[slot, p], sem.at[slot, 1]))

    def start_gather(seq, slot):
        for p in range(n_pages):
            for cp in page_copies(seq, slot, p):
                cp.start()

    @pl.when(b == 0)
    def _():
        start_gather(0, 0)

    @pl.when(b + 1 < pl.num_programs(0))
    def _():
        start_gather(b + 1, 1 - slot)

    for p in range(n_pages):
        for cp in page_copies(b, slot, p):
            cp.wait()

    ql, qr = ql_ref[...], qr_ref[...]
    cn = cnew_ref[...].astype(BF16).astype(F32)
    kn = krnew_ref[...].astype(BF16).astype(F32)
    s_new = (jnp.sum(ql.astype(F32) * cn, axis=-1, keepdims=True)
             + jnp.sum(qr.astype(F32) * kn, axis=-1, keepdims=True)) * scale

    pages = [ckv_buf[slot, p].astype(BF16) for p in range(n_pages)]
    s = jnp.concatenate(
        [_dot_nt(ql, pages[p]) + _dot(qr, krt_buf[slot, p].astype(BF16)) for p in range(n_pages)],
        axis=1) * scale
    m = jnp.maximum(jnp.max(s, axis=-1, keepdims=True), s_new)
    e = jnp.exp(s - m)
    e_new = jnp.exp(s_new - m)
    eb = e.astype(BF16)
    page = pages[0].shape[0]
    acc = e_new.astype(BF16).astype(F32) * cn
    for p in range(n_pages):
        acc += _dot(eb[:, p * page:(p + 1) * page], pages[p])
    o_ref[...] = (acc / (jnp.sum(e, axis=-1, keepdims=True) + e_new)).astype(o_ref.dtype)


def _sample_attention(q_lat, q_rope, c_new, kr_new, cache_ckv, cache_krt, page_table, j, scale):
    n, n_heads, kv_lora = q_lat.shape
    rope_dim = q_rope.shape[-1]
    n_pages, page = page_table.shape[1], cache_ckv.shape[2]
    per_row = lambda s1, w: pl.BlockSpec((None, s1, w), lambda b, pt: (b, 0, 0))
    hbm = pl.BlockSpec(memory_space=pl.ANY)
    grid_spec = pltpu.PrefetchScalarGridSpec(
        num_scalar_prefetch=1,
        grid=(n,),
        in_specs=[per_row(n_heads, kv_lora), per_row(n_heads, rope_dim), per_row(1, kv_lora), per_row(1, rope_dim),
                  hbm, hbm],
        out_specs=per_row(n_heads, kv_lora),
        scratch_shapes=[pltpu.VMEM((2, n_pages, page, kv_lora), cache_ckv.dtype),
                        pltpu.VMEM((2, n_pages, rope_dim, page), cache_krt.dtype),
                        pltpu.SemaphoreType.DMA((2, 2))],
    )
    return pl.pallas_call(
        functools.partial(_sample_attn_kernel, scale=scale, layer=j, n_pages=n_pages),
        grid_spec=grid_spec,
        out_shape=jax.ShapeDtypeStruct((n, n_heads, kv_lora), BF16),
        compiler_params=_params("arbitrary"),
        name="sample_attn",
    )(page_table.reshape(-1), q_lat, q_rope, c_new, kr_new, cache_ckv, cache_krt)


def _proj_ln_kernel(a_ref, x_ref, w_ref, g_ref, b_ref, o_ref, *, alpha):
    y = alpha * x_ref[...] + _dot(a_ref[...], w_ref[...])
    o_ref[...] = _layer_norm(y, g_ref[...], b_ref[...])


def _proj_ln(a, x, w, g, b, j, layer, alpha, tm_pref=512):
    m, d = x.shape
    k = a.shape[1]
    tm = _tile(m, tm_pref)
    return pl.pallas_call(
        functools.partial(_proj_ln_kernel, alpha=alpha),
        grid=(m // tm,),
        in_specs=[
            pl.BlockSpec((tm, k), lambda i: (i, 0)),
            pl.BlockSpec((tm, d), lambda i: (i, 0)),
            _resident((None, k, d), lambda i: (j, 0, 0)),
            pl.BlockSpec((None, None, 1, d), lambda i: (layer, 1, 0, 0)),
            pl.BlockSpec((None, None, 1, d), lambda i: (layer, 1, 0, 0)),
        ],
        out_specs=pl.BlockSpec((tm, d), lambda i: (i, 0)),
        out_shape=jax.ShapeDtypeStruct((m, d), F32),
        compiler_params=_params("parallel"),
        name="proj_ln",
    )(a, x, w, g, b)


def _rope_tables(pos, rope_dim):
    inv = ROPE_THETA ** (-jnp.arange(0, rope_dim, 2, dtype=F32) / rope_dim)
    ang = pos.astype(F32)[:, None] * inv[None, :]
    cos, sin = jnp.cos(ang), jnp.sin(ang)
    reps = LANES // rope_dim
    return jnp.tile(jnp.concatenate([cos, cos], -1), (1, reps)), jnp.tile(jnp.concatenate([-sin, sin], -1), (1, reps))


def _swap_halves(w):
    half = w.shape[-1] // 2
    return jnp.concatenate([w[..., half:], w[..., :half]], axis=-1)


def _pad_rows(a, rows):
    return jnp.pad(a, ((0, rows - a.shape[0]), (0, 0)))


def kernel(x_prompt, x_sample, state_conv, cache_kv_latent, cache_k_rope, page_table, meta_tokens, ln_g, ln_b, w_ffn_gate, w_ffn_up, w_ffn_down, w_conv_in, w_conv, w_conv_out, w_q_a, g_q_a, w_q_b, w_kv_a, g_kv_a, w_uk, w_uv, w_o):
    nb, seq, d = x_prompt.shape
    ns, dec_seq, _ = x_sample.shape
    assert dec_seq == 1, "the sample path handles one new token per sequence"
    n_meta = meta_tokens.shape[0]
    depth = ln_g.shape[0]
    alpha = float((2 * depth) ** 0.25)
    n_heads = w_q_b.shape[2]
    rope_dim = cache_k_rope.shape[-1]
    nope_dim = w_q_b.shape[3] - rope_dim
    kv_lora = w_uk.shape[1]
    scale = float((nope_dim + rope_dim) ** -0.5)
    past = page_table.shape[1] * cache_kv_latent.shape[2]
    assert nope_dim == LANES and w_uv.shape[-1] == LANES and n_heads % 2 == 0 and n_meta <= LANES

    xm = x_prompt.reshape(nb * seq, d)
    xe = jnp.concatenate([x_sample.reshape(ns, d), meta_tokens.astype(x_prompt.dtype)], axis=0)
    n_ext = ns + n_meta

    wqa = w_q_a.astype(BF16)
    wqn = w_q_b[..., :nope_dim].reshape(w_q_b.shape[0], w_q_b.shape[1], n_heads * nope_dim).astype(BF16)
    wq_rope = w_q_b[..., nope_dim:]
    wqr = wq_rope.reshape(w_q_b.shape[0], w_q_b.shape[1], n_heads * rope_dim).astype(BF16)
    wqrs = _swap_halves(wq_rope).reshape(wqr.shape).astype(BF16)
    wk_rope = w_kv_a[..., kv_lora:]
    zpad = jnp.zeros(wk_rope.shape[:-1] + (LANES - rope_dim,), w_kv_a.dtype)
    wkv = jnp.concatenate([w_kv_a[..., :kv_lora], wk_rope, zpad, _swap_halves(wk_rope), zpad], axis=-1).astype(BF16)
    wuk = w_uk.reshape(w_uk.shape[0], kv_lora, n_heads * nope_dim).astype(BF16)
    wuv = w_uv.reshape(w_uv.shape[0], kv_lora, -1).astype(BF16)
    wo = w_o.astype(BF16)
    ln_g4, ln_b4 = ln_g[:, :, None, :], ln_b[:, :, None, :]
    g_q3, g_kv3 = g_q_a[:, None, :], g_kv_a[:, None, :]
    cache_krt = jnp.swapaxes(cache_k_rope, 2, 3)

    cos_m, sin_m = _rope_tables(jnp.tile(n_meta + jnp.arange(seq), nb), rope_dim)
    cos_e, sin_e = _rope_tables(jnp.concatenate([jnp.full((ns,), past), jnp.arange(n_meta)]), rope_dim)

    ffn_w4 = (w_ffn_gate, w_ffn_up, w_ffn_down)
    piggyback = _can_cast_next(nb * seq, d, w_ffn_gate.shape[-1])
    ready = {}

    def ffn_both(xm, xe, layer, half, ln_idx):
        if (layer, half) in ready:
            w = ready.pop((layer, half))
            xe = _ffn(xe, *w, ln_g4, ln_b4, layer, ln_idx, alpha)
        else:
            xe, *w = _ffn_cast(xe, *ffn_w4, ln_g4, ln_b4, layer, half, ln_idx, alpha)
        nxt = (layer, 1) if half == 0 else (layer + 1, 0)
        if piggyback and nxt[0] < depth:
            xm, *w_next = _ffn(xm, *w, ln_g4, ln_b4, layer, ln_idx, alpha, cast_next=ffn_w4 + nxt)
            ready[nxt] = w_next
        else:
            xm = _ffn(xm, *w, ln_g4, ln_b4, layer, ln_idx, alpha)
        return xm, xe

    conv_p, conv_s, ckv_p, ckv_s, kr_p, kr_s = [], [], [], [], [], []
    for i in range(depth):
        j = i // N_MIXERS
        xm, xe = ffn_both(xm, xe, i, 0, 0)
        if i % N_MIXERS == 0:
            dc = w_conv_out.shape[1]
            st = state_conv[j]
            zeros = jnp.zeros((n_meta, dc), F32)
            p1 = jnp.concatenate([st[:, 1], zeros], axis=0)
            p2 = jnp.concatenate([st[:, 0], zeros], axis=0)
            rows = jnp.arange(n_ext)[:, None]
            f1 = (rows <= ns).astype(F32)
            f2 = (rows <= ns + 1).astype(F32)
            xe, u_e, wb, wc, wh, wout = _conv_mixer_rows(xe, p1, p2, f1, f2, w_conv_in, w_conv, w_conv_out,
                                                         ln_g4, ln_b4, j, i, alpha)
            prev = jnp.broadcast_to(jnp.pad(u_e[n_ext - 2:], ((SUBLANES - 2, 0), (0, 0)))[None], (nb, SUBLANES, dc))
            xm, u_last = _conv_mixer_seq(xm, prev, wb, wc, wh, w_conv, wout, ln_g4, ln_b4, j, i, alpha, seq)
            tiles_per_seq = u_last.shape[0] // nb
            conv_p.append(u_last[tiles_per_seq - 1::tiles_per_seq, SUBLANES - 2:])
            conv_s.append(jnp.stack([st[:, 1], u_e[:ns]], axis=1))
        else:
            proj = functools.partial(_mla_proj, wqa=wqa, gqa=g_q3, wqn=wqn, wqr=wqr, wqrs=wqrs, wkv=wkv, gkv=g_kv3,
                                     wuk=wuk, wuv=wuv, j=j, n_heads=n_heads)
            qc_e, kc_e, v_e, ckv_e, kr_e = proj(xe, cos_e, sin_e)
            qc_m, kc_m, v_m, ckv_m, kr_m = proj(xm, cos_m, sin_m)
            o_meta = _prefix_attention(qc_e[ns:], kc_e[ns:], v_e[ns:], n_heads, scale)
            o_main = _attention(qc_m, kc_m, v_m, _pad_rows(kc_e[ns:], LANES), _pad_rows(v_e[ns:], LANES), n_meta,
                                nb, seq, n_heads, scale)
            q_lat = _absorb_q(qc_e, wuk, j, ns, n_heads).transpose(1, 0, 2)
            q_rope_s = qc_e[:ns].reshape(ns, n_heads, 2 * LANES)[:, :, LANES:LANES + rope_dim]
            o_lat = _sample_attention(q_lat, q_rope_s, ckv_e[:ns, None], kr_e[:ns, None],
                                      cache_kv_latent, cache_krt, page_table, j, scale)
            o_s = _expand_o(o_lat.transpose(1, 0, 2), wuv, j, n_heads)
            o_e = jnp.concatenate([o_s, o_meta], axis=0)
            xe = _proj_ln(o_e, xe, wo, ln_g4, ln_b4, j, i, alpha)
            xm = _proj_ln(o_main, xm, wo, ln_g4, ln_b4, j, i, alpha)
            meta_rows = lambda a: jnp.broadcast_to(a[ns:][None], (nb, n_meta, a.shape[-1]))
            ckv_p.append(jnp.concatenate([meta_rows(ckv_e), ckv_m.reshape(nb, seq, kv_lora)], axis=1))
            kr_p.append(jnp.concatenate([meta_rows(kr_e), kr_m.reshape(nb, seq, rope_dim)], axis=1))
            ckv_s.append(ckv_e[:ns, None])
            kr_s.append(kr_e[:ns, None])
        xm, xe = ffn_both(xm, xe, i, 1, 2)

    y_prompt = xm.reshape(nb, seq, d)
    y_sample = xe[:ns].reshape(ns, dec_seq, d)
    return (y_prompt, y_sample, jnp.stack(conv_p), jnp.stack(conv_s), jnp.stack(ckv_p), jnp.stack(ckv_s),
            jnp.stack(kr_p), jnp.stack(kr_s))
```

```python
import functools

import jax
import jax.numpy as jnp
from jax import lax
from jax.experimental import pallas as pl
from jax.experimental.pallas import tpu as pltpu

LN_EPS = 1e-5
RMS_EPS = 1e-6
ROPE_THETA = 10000.0
N_MIXERS = 2
LOG2_E = 1.4426950408889634
FFN_SUB = 256

LANES = 128
SUBLANES = 8
VMEM_LIMIT_BYTES = 56 * 1024 * 1024

BF16 = jnp.bfloat16
F32 = jnp.float32


def _tile(n, pref):
    if n <= pref:
        return n
    for t in range(pref, 15, -1):
        if n % t == 0 and t % 16 == 0:
            return t
    return n


def _params(*sem):
    return pltpu.CompilerParams(dimension_semantics=sem, vmem_limit_bytes=VMEM_LIMIT_BYTES)


def _resident(shape, index_map):
    return pl.BlockSpec(shape, index_map, pipeline_mode=pl.Buffered(1))


def _layer_norm(y, g, b):
    mu = jnp.mean(y, axis=-1, keepdims=True)
    yc = y - mu
    var = jnp.mean(yc * yc, axis=-1, keepdims=True)
    return yc * lax.rsqrt(var + LN_EPS) * g + b


def _rms_norm(y, g):
    return y * lax.rsqrt(jnp.mean(y * y, axis=-1, keepdims=True) + RMS_EPS) * g


def _dot(a, b):
    return jnp.dot(a, b, preferred_element_type=F32)


def _dot_nt(a, b):
    return lax.dot_general(a, b, (((1,), (1,)), ((), ())), preferred_element_type=F32)


def _ffn_chunk(xb, wg_ref, wu_ref, wd_ref):
    tf = wg_ref.shape[1]
    sub = FFN_SUB if tf % FFN_SUB == 0 else tf
    part = None
    for c in range(tf // sub):
        cols = slice(c * sub, (c + 1) * sub)
        gate = _dot(xb, wg_ref[:, cols])
        up = _dot(xb, wu_ref[:, cols])
        h = gate / (1.0 + jnp.exp(-gate)) * up
        d = _dot(h.astype(BF16), wd_ref[cols, :])
        part = d if part is None else part + d
    return part


def _ffn_step(f, n_f, x_ref, wg_ref, wu_ref, wd_ref, g_ref, b_ref, o_ref, xb_ref, acc_ref, alpha):
    @pl.when(f == 0)
    def _():
        xb_ref[...] = x_ref[...].astype(BF16)
        acc_ref[...] = jnp.zeros_like(acc_ref)

    acc_ref[...] += _ffn_chunk(xb_ref[...], wg_ref, wu_ref, wd_ref)

    @pl.when(f == n_f - 1)
    def _():
        y = alpha * x_ref[...] + 0.5 * acc_ref[...]
        o_ref[...] = _layer_norm(y, g_ref[...], b_ref[...])


def _ffn_deferred_ln_body(x_ref, wg_ref, wu_ref, wd_ref, g_ref, b_ref, o_ref, xb_ref, acc_ref, y_ref,
                          *, alpha, n_i, n_f):
    s = pl.program_id(0)
    i, f = s // n_f, s % n_f

    @pl.when(s == 0)
    def _():
        y_ref[...] = jnp.zeros_like(y_ref)

    @pl.when(jnp.logical_and(f == 0, i < n_i))
    def _():
        o_ref[...] = _layer_norm(y_ref[...], g_ref[...], b_ref[...])
        xb = x_ref[...].astype(BF16)
        xb_ref[...] = xb
        part = _ffn_chunk(xb, wg_ref, wu_ref, wd_ref)
        if n_f == 1:
            y_ref[...] = alpha * x_ref[...] + 0.5 * part
        else:
            acc_ref[...] = part

    @pl.when(i == n_i)
    def _():
        o_ref[...] = _layer_norm(y_ref[...], g_ref[...], b_ref[...])

    @pl.when(jnp.logical_and(f != 0, f < n_f - 1))
    def _():
        acc_ref[...] += _ffn_chunk(xb_ref[...], wg_ref, wu_ref, wd_ref)

    @pl.when(jnp.logical_and(f != 0, f == n_f - 1))
    def _():
        acc = acc_ref[...] + _ffn_chunk(xb_ref[...], wg_ref, wu_ref, wd_ref)
        y_ref[...] = alpha * x_ref[...] + 0.5 * acc


def _ffn_kernel(x_ref, wg_ref, wu_ref, wd_ref, g_ref, b_ref, o_ref, xb_ref, acc_ref, y_ref, **kw):
    _ffn_deferred_ln_body(x_ref, wg_ref, wu_ref, wd_ref, g_ref, b_ref, o_ref, xb_ref, acc_ref, y_ref, **kw)


def _ffn_cast_kernel(x_ref, wg_ref, wu_ref, wd_ref, g_ref, b_ref, o_ref, wgb_ref, wub_ref, wdb_ref,
                     xb_ref, acc_ref, *, alpha):
    wgb_ref[...] = wg_ref[...].astype(BF16)
    wub_ref[...] = wu_ref[...].astype(BF16)
    wdb_ref[...] = wd_ref[...].astype(BF16)
    _ffn_step(pl.program_id(0), pl.num_programs(0), x_ref, wgb_ref, wub_ref, wdb_ref,
              g_ref, b_ref, o_ref, xb_ref, acc_ref, alpha)


def _ffn_cast_next_kernel(x_ref, wg_ref, wu_ref, wd_ref, g_ref, b_ref, ng_ref, nu_ref, nd_ref,
                          o_ref, ngb_ref, nub_ref, ndb_ref, xb_ref, acc_ref, y_ref, **kw):
    ngb_ref[...] = ng_ref[...].astype(BF16)
    nub_ref[...] = nu_ref[...].astype(BF16)
    ndb_ref[...] = nd_ref[...].astype(BF16)
    _ffn_deferred_ln_body(x_ref, wg_ref, wu_ref, wd_ref, g_ref, b_ref, o_ref, xb_ref, acc_ref, y_ref, **kw)


def _ffn_tiles(m, d, ff, tm_pref=512, tf_pref=512):
    return _tile(m, tm_pref), _tile(ff, tf_pref)


def _can_cast_next(m, d, ff):
    tm, _ = _ffn_tiles(m, d, ff)
    n_i = m // tm
    return d % n_i == 0 and (d // n_i) % LANES == 0


def _ffn(x, wg, wu, wd, g, b, layer, ln_idx, alpha, cast_next=None):
    m, d = x.shape
    ff = wg.shape[-1]
    tm, tf = _ffn_tiles(m, d, ff)
    n_i, n_f = m // tm, ff // tf
    last = n_i * n_f
    row = lambda s: jnp.minimum(s // n_f, n_i - 1)
    chunk = lambda s: jnp.where(s == last, n_f - 1, s % n_f)
    prev_row = lambda s: jnp.maximum(s // n_f - 1, 0)
    in_specs = [
        pl.BlockSpec((tm, d), lambda s: (row(s), 0)),
        pl.BlockSpec((d, tf), lambda s: (0, chunk(s))),
        pl.BlockSpec((d, tf), lambda s: (0, chunk(s))),
        pl.BlockSpec((tf, d), lambda s: (chunk(s), 0)),
        pl.BlockSpec((None, None, 1, d), lambda s: (layer, ln_idx, 0, 0)),
        pl.BlockSpec((None, None, 1, d), lambda s: (layer, ln_idx, 0, 0)),
    ]
    out_spec = pl.BlockSpec((tm, d), lambda s: (prev_row(s), 0))
    out_shape = jax.ShapeDtypeStruct((m, d), F32)
    scratch = [pltpu.VMEM((tm, d), BF16), pltpu.VMEM((tm, d), F32), pltpu.VMEM((tm, d), F32)]
    kw = dict(alpha=alpha, n_i=n_i, n_f=n_f)
    if cast_next is None:
        return pl.pallas_call(
            functools.partial(_ffn_kernel, **kw),
            grid=(last + 1,),
            in_specs=in_specs,
            out_specs=out_spec,
            out_shape=out_shape,
            scratch_shapes=scratch,
            compiler_params=_params("arbitrary"),
            name="ffn_macaron",
        )(x, wg, wu, wd, g, b)
    ng, nu, nd, layer_n, half_n = cast_next
    td = d // n_i
    return pl.pallas_call(
        functools.partial(_ffn_cast_next_kernel, **kw),
        grid=(last + 1,),
        in_specs=in_specs + [
            pl.BlockSpec((None, None, td, tf), lambda s: (layer_n, half_n, row(s), chunk(s))),
            pl.BlockSpec((None, None, td, tf), lambda s: (layer_n, half_n, row(s), chunk(s))),
            pl.BlockSpec((None, None, tf, td), lambda s: (layer_n, half_n, chunk(s), row(s))),
        ],
        out_specs=[out_spec,
                   pl.BlockSpec((td, tf), lambda s: (row(s), chunk(s))),
                   pl.BlockSpec((td, tf), lambda s: (row(s), chunk(s))),
                   pl.BlockSpec((tf, td), lambda s: (chunk(s), row(s)))],
        out_shape=[out_shape, jax.ShapeDtypeStruct((d, ff), BF16), jax.ShapeDtypeStruct((d, ff), BF16),
                   jax.ShapeDtypeStruct((ff, d), BF16)],
        scratch_shapes=scratch,
        compiler_params=_params("arbitrary"),
        name="ffn_macaron_cast_next",
    )(x, wg, wu, wd, g, b, ng, nu, nd)


def _ffn_cast(x, wg, wu, wd, g, b, layer, half, ln_idx, alpha, tf_pref=256):
    m, d = x.shape
    ff = wg.shape[-1]
    tf = _tile(ff, tf_pref)
    return pl.pallas_call(
        functools.partial(_ffn_cast_kernel, alpha=alpha),
        grid=(ff // tf,),
        in_specs=[
            pl.BlockSpec((m, d), lambda f: (0, 0)),
            pl.BlockSpec((None, None, d, tf), lambda f: (layer, half, 0, f)),
            pl.BlockSpec((None, None, d, tf), lambda f: (layer, half, 0, f)),
            pl.BlockSpec((None, None, tf, d), lambda f: (layer, half, f, 0)),
            pl.BlockSpec((None, None, 1, d), lambda f: (layer, ln_idx, 0, 0)),
            pl.BlockSpec((None, None, 1, d), lambda f: (layer, ln_idx, 0, 0)),
        ],
        out_specs=[
            pl.BlockSpec((m, d), lambda f: (0, 0)),
            pl.BlockSpec((d, tf), lambda f: (0, f)),
            pl.BlockSpec((d, tf), lambda f: (0, f)),
            pl.BlockSpec((tf, d), lambda f: (f, 0)),
        ],
        out_shape=[jax.ShapeDtypeStruct((m, d), F32), jax.ShapeDtypeStruct((d, ff), BF16),
                   jax.ShapeDtypeStruct((d, ff), BF16), jax.ShapeDtypeStruct((ff, d), BF16)],
        scratch_shapes=[pltpu.VMEM((m, d), BF16), pltpu.VMEM((m, d), F32)],
        compiler_params=_params("arbitrary"),
        name="ffn_macaron_cast",
    )(x, wg, wu, wd, g, b)


def _conv_taps(u, u_m1, u_m2, wconv_ref):
    w = wconv_ref[...]
    return w[0:1] * u_m2 + w[1:2] * u_m1 + w[2:3] * u


def _conv_seq_kernel(x_ref, prev_ref, wb_ref, wc_ref, wh_ref, wconv_ref, wout_ref, g_ref, b_ref,
                     o_ref, ulast_ref, xb_ref, acc_ref, carry_ref, *, alpha, tiles_per_seq):
    i, c = pl.program_id(0), pl.program_id(1)
    tm = x_ref.shape[0]

    @pl.when(c == 0)
    def _():
        xb_ref[...] = x_ref[...].astype(BF16)
        acc_ref[...] = jnp.zeros_like(acc_ref)

    @pl.when(i == 0)
    def _():
        carry_ref[c] = jnp.zeros(carry_ref.shape[1:], F32)

    xb = xb_ref[...]
    u = _dot(xb, wc_ref[...]) * _dot(xb, wh_ref[...])
    seq_start = (i % tiles_per_seq) == 0
    tail = jnp.where(seq_start, prev_ref[...], carry_ref[c])
    row = lax.broadcasted_iota(jnp.int32, u.shape, 0)
    u_m1 = jnp.where(row == 0, tail[7:8], pltpu.roll(u, 1, 0))
    u_m2 = jnp.where(row == 0, tail[6:7], jnp.where(row == 1, tail[7:8], pltpu.roll(u, 2, 0)))
    conv = _conv_taps(u, u_m1, u_m2, wconv_ref)
    last = u[tm - SUBLANES:, :]
    carry_ref[c] = last
    ulast_ref[...] = last
    z = _dot(xb, wb_ref[...]) * conv
    acc_ref[...] += _dot(z.astype(BF16), wout_ref[...])

    @pl.when(c == pl.num_programs(1) - 1)
    def _():
        o_ref[...] = _layer_norm(alpha * x_ref[...] + acc_ref[...], g_ref[...], b_ref[...])


def _conv_rows_kernel(x_ref, p1_ref, p2_ref, f1_ref, f2_ref, wb_ref, wc_ref, wh_ref, wconv_ref, wout_ref,
                      g_ref, b_ref, o_ref, u_ref, wbb_ref, wcb_ref, whb_ref, woutb_ref, xb_ref, acc_ref, *, alpha):
    c = pl.program_id(0)

    @pl.when(c == 0)
    def _():
        xb_ref[...] = x_ref[...].astype(BF16)
        acc_ref[...] = jnp.zeros_like(acc_ref)

    wb, wc, wh = wb_ref[...].astype(BF16), wc_ref[...].astype(BF16), wh_ref[...].astype(BF16)
    wout = wout_ref[...].astype(BF16)
    wbb_ref[...] = wb
    wcb_ref[...] = wc
    whb_ref[...] = wh
    woutb_ref[...] = wout

    xb = xb_ref[...]
    u = _dot(xb, wc) * _dot(xb, wh)
    u_m1 = jnp.where(f1_ref[...] > 0.5, p1_ref[...], pltpu.roll(u, 1, 0))
    u_m2 = jnp.where(f2_ref[...] > 0.5, p2_ref[...], pltpu.roll(u, 2, 0))
    conv = _conv_taps(u, u_m1, u_m2, wconv_ref)
    u_ref[...] = u
    z = _dot(xb, wb) * conv
    acc_ref[...] += _dot(z.astype(BF16), wout)

    @pl.when(c == pl.num_programs(0) - 1)
    def _():
        o_ref[...] = _layer_norm(alpha * x_ref[...] + acc_ref[...], g_ref[...], b_ref[...])


def _conv_mixer_seq(x, prev, wb, wc, wh, w_conv, w_out, g, b, j, layer, alpha, seq_len, tm_pref=512, tn_pref=512):
    m, d = x.shape
    dc = w_out.shape[0]
    tm, tn = _tile(seq_len, tm_pref), _tile(dc, tn_pref)
    tiles_per_seq = seq_len // tm
    n_c = dc // tn
    w_spec = pl.BlockSpec((d, tn), lambda i, c: (0, c))
    return pl.pallas_call(
        functools.partial(_conv_seq_kernel, alpha=alpha, tiles_per_seq=tiles_per_seq),
        grid=(m // tm, n_c),
        in_specs=[
            pl.BlockSpec((tm, d), lambda i, c: (i, 0)),
            pl.BlockSpec((None, SUBLANES, tn), lambda i, c: (i // tiles_per_seq, 0, c)),
            w_spec, w_spec, w_spec,
            pl.BlockSpec((None, 3, tn), lambda i, c: (j, 0, c)),
            pl.BlockSpec((tn, d), lambda i, c: (c, 0)),
            pl.BlockSpec((None, None, 1, d), lambda i, c: (layer, 1, 0, 0)),
            pl.BlockSpec((None, None, 1, d), lambda i, c: (layer, 1, 0, 0)),
        ],
        out_specs=[
            pl.BlockSpec((tm, d), lambda i, c: (i, 0)),
            pl.BlockSpec((None, SUBLANES, tn), lambda i, c: (i, 0, c)),
        ],
        out_shape=[jax.ShapeDtypeStruct((m, d), F32),
                   jax.ShapeDtypeStruct((m // tm, SUBLANES, dc), F32)],
        scratch_shapes=[pltpu.VMEM((tm, d), BF16), pltpu.VMEM((tm, d), F32),
                        pltpu.VMEM((n_c, SUBLANES, tn), F32)],
        compiler_params=_params("arbitrary", "arbitrary"),
        name="conv_mixer_seq",
    )(x, prev, wb, wc, wh, w_conv, w_out, g, b)


def _conv_mixer_rows(x, p1, p2, f1, f2, w_in, w_conv, w_out, g, b, j, layer, alpha, tn_pref=256):
    m, d = x.shape
    dc = w_out.shape[1]
    tn = _tile(dc, tn_pref)
    n_c = dc // tn
    w_spec = lambda part: pl.BlockSpec((None, d, tn), lambda c: (j, 0, part * n_c + c))
    wb_out = pl.BlockSpec((d, tn), lambda c: (0, c))
    return pl.pallas_call(
        functools.partial(_conv_rows_kernel, alpha=alpha),
        grid=(n_c,),
        in_specs=[
            pl.BlockSpec((m, d), lambda c: (0, 0)),
            pl.BlockSpec((m, tn), lambda c: (0, c)),
            pl.BlockSpec((m, tn), lambda c: (0, c)),
            pl.BlockSpec((m, 1), lambda c: (0, 0)),
            pl.BlockSpec((m, 1), lambda c: (0, 0)),
            w_spec(0), w_spec(1), w_spec(2),
            pl.BlockSpec((None, 3, tn), lambda c: (j, 0, c)),
            pl.BlockSpec((None, tn, d), lambda c: (j, c, 0)),
            pl.BlockSpec((None, None, 1, d), lambda c: (layer, 1, 0, 0)),
            pl.BlockSpec((None, None, 1, d), lambda c: (layer, 1, 0, 0)),
        ],
        out_specs=[
            pl.BlockSpec((m, d), lambda c: (0, 0)),
            pl.BlockSpec((m, tn), lambda c: (0, c)),
            wb_out, wb_out, wb_out,
            pl.BlockSpec((tn, d), lambda c: (c, 0)),
        ],
        out_shape=[jax.ShapeDtypeStruct((m, d), F32), jax.ShapeDtypeStruct((m, dc), F32),
                   jax.ShapeDtypeStruct((d, dc), BF16), jax.ShapeDtypeStruct((d, dc), BF16),
                   jax.ShapeDtypeStruct((d, dc), BF16), jax.ShapeDtypeStruct((dc, d), BF16)],
        scratch_shapes=[pltpu.VMEM((m, d), BF16), pltpu.VMEM((m, d), F32)],
        compiler_params=_params("arbitrary"),
        name="conv_mixer_rows",
    )(x, p1, p2, f1, f2, w_in, w_in, w_in, w_conv, w_out, g, b)


def _mla_proj_kernel(x_ref, cos_ref, sin_ref, wqa_ref, gqa_ref, wqn_ref, wqr_ref, wqrs_ref, wkv_ref, gkv_ref,
                     wuk_ref, wuv_ref, qc_ref, kc_ref, v_ref, ckv_ref, kr_ref, *, n_heads, kv_lora, rope_dim):
    half = LANES // 2
    assert rope_dim == half, "rope halves are packed two heads per 128 lanes"
    xb = x_ref[...].astype(BF16)
    cos, sin = cos_ref[...], sin_ref[...]
    lane = lax.broadcasted_iota(jnp.int32, cos.shape, 1)
    low = lane < half

    cq = _rms_norm(_dot(xb, wqa_ref[...]), gqa_ref[...]).astype(BF16)
    qn = _dot(cq, wqn_ref[...])
    qr_a = _dot(cq, wqr_ref[...])
    qr_b = _dot(cq, wqrs_ref[...])
    for pair in range(n_heads // 2):
        sl = slice(pair * LANES, (pair + 1) * LANES)
        both = qr_a[:, sl] * cos + qr_b[:, sl] * sin
        for k in range(2):
            h = 2 * pair + k
            part = both if k == 0 else pltpu.roll(both, half, 1)
            qc_ref[:, h * 2 * LANES:h * 2 * LANES + LANES] = qn[:, h * LANES:(h + 1) * LANES].astype(BF16)
            qc_ref[:, h * 2 * LANES + LANES:(h + 1) * 2 * LANES] = jnp.where(low, part, 0.0).astype(BF16)

    kv = _dot(xb, wkv_ref[...])
    ckv = _rms_norm(kv[:, :kv_lora], gkv_ref[...])
    ckv_ref[...] = ckv
    kr = kv[:, kv_lora:kv_lora + LANES] * cos + kv[:, kv_lora + LANES:] * sin
    kr = jnp.where(low, kr, 0.0)
    kr_ref[...] = kr[:, :rope_dim]
    ckv_b = ckv.astype(BF16)
    kn = _dot(ckv_b, wuk_ref[...])
    v = _dot(ckv_b, wuv_ref[...])
    kr_b = kr.astype(BF16)
    ones = jnp.ones((xb.shape[0], LANES), BF16)
    for h in range(n_heads):
        kc_ref[:, h * 2 * LANES:h * 2 * LANES + LANES] = kn[:, h * LANES:(h + 1) * LANES].astype(BF16)
        kc_ref[:, h * 2 * LANES + LANES:(h + 1) * 2 * LANES] = kr_b
        v_ref[:, h * 2 * LANES:h * 2 * LANES + LANES] = v[:, h * LANES:(h + 1) * LANES].astype(BF16)
        v_ref[:, h * 2 * LANES + LANES:(h + 1) * 2 * LANES] = ones


def _mla_proj(x, cos, sin, wqa, gqa, wqn, wqr, wqrs, wkv, gkv, wuk, wuv, j, n_heads, tm_pref=256):
    m, d = x.shape
    kv_lora = wuk.shape[1]
    rope_dim = wqr.shape[-1] // n_heads
    tm = _tile(m, tm_pref)
    hq = n_heads * 2 * LANES
    hv = hq
    row = lambda w: pl.BlockSpec((tm, w), lambda i: (i, 0))
    full = lambda a: _resident((None,) + a.shape[1:], lambda i: (j,) + (0,) * (a.ndim - 1))
    return pl.pallas_call(
        functools.partial(_mla_proj_kernel, n_heads=n_heads, kv_lora=kv_lora, rope_dim=rope_dim),
        grid=(m // tm,),
        in_specs=[row(d), row(LANES), row(LANES), full(wqa), full(gqa), full(wqn), full(wqr), full(wqrs),
                  full(wkv), full(gkv), full(wuk), full(wuv)],
        out_specs=[row(hq), row(hq), row(hv), row(kv_lora), row(rope_dim)],
        out_shape=[jax.ShapeDtypeStruct((m, hq), BF16), jax.ShapeDtypeStruct((m, hq), BF16),
                   jax.ShapeDtypeStruct((m, hv), BF16), jax.ShapeDtypeStruct((m, kv_lora), F32),
                   jax.ShapeDtypeStruct((m, rope_dim), F32)],
        compiler_params=_params("parallel"),
        name="mla_proj",
    )(x, cos, sin, wqa, gqa, wqn, wqr, wqrs, wkv, gkv, wuk, wuv)


def _lane_chunks(s):
    return [s[:, c * LANES:(c + 1) * LANES] for c in range(s.shape[1] // LANES)]


def _row_max(parts):
    m = functools.reduce(jnp.maximum, parts)
    return jnp.broadcast_to(jnp.max(m, axis=1, keepdims=True), m.shape)


def _attn_kernel(q_ref, k_ref, v_ref, kp_ref, vp_ref, o_ref, m_ref, acc_ref, *, c_exp, tk, n_pre):
    qi = pl.program_id(2)
    dq = 2 * LANES
    n_grp = q_ref.shape[1] // dq
    assert q_ref.shape[0] == tk and v_ref.shape[1] == n_grp * dq
    qs = [q_ref[:, h * dq:(h + 1) * dq] for h in range(n_grp)]
    hq = lambda h: slice(h * dq, (h + 1) * dq)
    k_blk = lambda h, start: k_ref[pl.ds(start, tk), hq(h)]
    v_blk = lambda h, start: v_ref[pl.ds(start, tk), hq(h)]
    hl = lambda h: slice(h * LANES, (h + 1) * LANES)

    start = pl.multiple_of(qi * tk, tk)
    row = lax.broadcasted_iota(jnp.int32, (tk, tk), 0)
    col = lax.broadcasted_iota(jnp.int32, (tk, tk), 1)
    causal = col <= row
    real_pre = lax.broadcasted_iota(jnp.int32, (tk, LANES), 1) < n_pre
    for h in range(n_grp):
        s_d = jnp.where(causal, _dot_nt(qs[h], k_blk(h, start)), -jnp.inf)
        s_p = jnp.where(real_pre, _dot_nt(qs[h], kp_ref[:, hq(h)]), -jnp.inf)
        parts = _lane_chunks(s_d) + [s_p]
        m = _row_max(parts)
        p_parts = [jnp.exp2((x - m) * c_exp) for x in parts]
        m_ref[:, hl(h)] = m
        acc_ref[:, hq(h)] = (_dot(jnp.concatenate(p_parts[:-1], axis=1).astype(BF16), v_blk(h, start))
                             + _dot(p_parts[-1].astype(BF16), vp_ref[:, hq(h)]))

    def body(jb, carry):
        start = pl.multiple_of(jb * tk, tk)
        for h in range(n_grp):
            parts = _lane_chunks(_dot_nt(qs[h], k_blk(h, start)))
            m_old = m_ref[:, hl(h)]
            m_new = jnp.maximum(m_old, _row_max(parts))
            a = jnp.exp2((m_old - m_new) * c_exp)
            p = jnp.concatenate([jnp.exp2((x - m_new) * c_exp) for x in parts], axis=1).astype(BF16)
            acc_ref[:, hq(h)] = jnp.concatenate([a, a], axis=1) * acc_ref[:, hq(h)] + _dot(p, v_blk(h, start))
            m_ref[:, hl(h)] = m_new
        return carry

    lax.fori_loop(0, qi, body, 0)
    for h in range(n_grp):
        acc = acc_ref[:, hq(h)]
        o_ref[:, hl(h)] = (acc[:, :LANES] / acc[:, LANES:]).astype(o_ref.dtype)


def _attention(q, k, v, k_pre, v_pre, n_pre, n_seq, seq_len, n_heads, scale, t_pref=512, heads_per_step=4):
    dq = 2 * LANES
    t = _tile(seq_len, t_pref)
    n_t = seq_len // t
    g = max(x for x in range(1, heads_per_step + 1) if n_heads % x == 0)
    assert t % LANES == 0 and k_pre.shape[0] == LANES and v.shape[1] == n_heads * dq
    return pl.pallas_call(
        functools.partial(_attn_kernel, c_exp=scale * LOG2_E, tk=t, n_pre=n_pre),
        grid=(n_seq, n_heads // g, n_t),
        in_specs=[
            pl.BlockSpec((t, g * dq), lambda b, h, i: (b * n_t + i, h)),
            pl.BlockSpec((seq_len, g * dq), lambda b, h, i: (b, h)),
            pl.BlockSpec((seq_len, g * dq), lambda b, h, i: (b, h)),
            pl.BlockSpec((LANES, g * dq), lambda b, h, i: (0, h)),
            pl.BlockSpec((LANES, g * dq), lambda b, h, i: (0, h)),
        ],
        out_specs=pl.BlockSpec((t, g * LANES), lambda b, h, i: (b * n_t + i, h)),
        out_shape=jax.ShapeDtypeStruct((n_seq * seq_len, n_heads * LANES), BF16),
        scratch_shapes=[pltpu.VMEM((t, g * LANES), F32), pltpu.VMEM((t, g * dq), F32)],
        compiler_params=_params("parallel", "parallel", "arbitrary"),
        name="attn_causal",
    )(q, k, v, k_pre, v_pre)


def _prefix_attn_kernel(q_ref, k_ref, v_ref, o_ref, *, scale):
    s = _dot_nt(q_ref[...], k_ref[...]) * scale
    row = lax.broadcasted_iota(jnp.int32, s.shape, 0)
    col = lax.broadcasted_iota(jnp.int32, s.shape, 1)
    s = jnp.where(col <= row, s, -jnp.inf)
    p = jnp.exp(s - jnp.max(s, axis=1, keepdims=True))
    o = _dot(p.astype(BF16), v_ref[:, :LANES]) / jnp.sum(p, axis=1, keepdims=True)
    o_ref[...] = o.astype(o_ref.dtype)


def _prefix_attention(q, k, v, n_heads, scale):
    n = q.shape[0]
    dq = 2 * LANES
    blk = pl.BlockSpec((n, dq), lambda h: (0, h))
    return pl.pallas_call(
        functools.partial(_prefix_attn_kernel, scale=scale),
        grid=(n_heads,),
        in_specs=[blk, blk, blk],
        out_specs=pl.BlockSpec((n, LANES), lambda h: (0, h)),
        out_shape=jax.ShapeDtypeStruct((n, n_heads * LANES), BF16),
        compiler_params=_params("parallel"),
        name="attn_prefix",
    )(q, k, v)


def _absorb_q_kernel(q_ref, wuk_ref, o_ref):
    o_ref[...] = _dot_nt(q_ref[:, :LANES], wuk_ref[...]).astype(o_ref.dtype)


def _absorb_q(q_cat, wuk, j, n_rows, n_heads):
    kv_lora = wuk.shape[1]
    return pl.pallas_call(
        _absorb_q_kernel,
        grid=(n_heads,),
        in_specs=[pl.BlockSpec((n_rows, 2 * LANES), lambda h: (0, h)),
                  pl.BlockSpec((None, kv_lora, LANES), lambda h: (j, 0, h))],
        out_specs=pl.BlockSpec((None, n_rows, kv_lora), lambda h: (h, 0, 0)),
        out_shape=jax.ShapeDtypeStruct((n_heads, n_rows, kv_lora), BF16),
        compiler_params=_params("parallel"),
        name="absorb_q",
    )(q_cat, wuk)


def _expand_o_kernel(o_ref, wuv_ref, out_ref):
    out_ref[...] = _dot(o_ref[...], wuv_ref[...]).astype(out_ref.dtype)


def _expand_o(o_lat, wuv, j, n_heads):
    _, n_rows, kv_lora = o_lat.shape
    hv = wuv.shape[-1] // n_heads
    return pl.pallas_call(
        _expand_o_kernel,
        grid=(n_heads,),
        in_specs=[pl.BlockSpec((None, n_rows, kv_lora), lambda h: (h, 0, 0)),
                  pl.BlockSpec((None, kv_lora, hv), lambda h: (j, 0, h))],
        out_specs=pl.BlockSpec((n_rows, hv), lambda h: (0, h)),
        out_shape=jax.ShapeDtypeStruct((n_rows, n_heads * hv), BF16),
        compiler_params=_params("parallel"),
        name="expand_o",
    )(o_lat, wuv)


def _sample_attn_kernel(pt_ref, ql_ref, qr_ref, cnew_ref, krnew_ref, ckv_hbm, krt_hbm, o_ref,
                        ckv_buf, krt_buf, sem, *, scale, layer, n_pages):
    b = pl.program_id(0)
    slot = b % 2

    def page_copies(seq, slot, p):
        page_id = pt_ref[seq * n_pages + p]
        return (pltpu.make_async_copy(ckv_hbm.at[layer, page_id], ckv_buf.at[slot, p], sem.at[slot, 0]),
                pltpu.make_async_copy(krt_hbm.at[layer, page_id], krt_buf.at[slot, p], sem.at[slot, 1]))

    def start_gather(seq, slot):
        for p in range(n_pages):
            for cp in page_copies(seq, slot, p):
                cp.start()

    @pl.when(b == 0)
    def _():
        start_gather(0, 0)

    @pl.when(b + 1 < pl.num_programs(0))
    def _():
        start_gather(b + 1, 1 - slot)

    for p in range(n_pages):
        for cp in page_copies(b, slot, p):
            cp.wait()

    ql, qr = ql_ref[...], qr_ref[...]
    cn = cnew_ref[...].astype(BF16).astype(F32)
    kn = krnew_ref[...].astype(BF16).astype(F32)
    s_new = (jnp.sum(ql.astype(F32) * cn, axis=-1, keepdims=True)
             + jnp.sum(qr.astype(F32) * kn, axis=-1, keepdims=True)) * scale

    pages = [ckv_buf[slot, p].astype(BF16) for p in range(n_pages)]
    s = jnp.concatenate(
        [_dot_nt(ql, pages[p]) + _dot(qr, krt_buf[slot, p].astype(BF16)) for p in range(n_pages)],
        axis=1) * scale
    m = jnp.maximum(jnp.max(s, axis=-1, keepdims=True), s_new)
    e = jnp.exp(s - m)
    e_new = jnp.exp(s_new - m)
    eb = e.astype(BF16)
    page = pages[0].shape[0]
    acc = e_new.astype(BF16).astype(F32) * cn
    for p in range(n_pages):
        acc += _dot(eb[:, p * page:(p + 1) * page], pages[p])
    o_ref[...] = (acc / (jnp.sum(e, axis=-1, keepdims=True) + e_new)).astype(o_ref.dtype)


def _sample_attention(q_lat, q_rope, c_new, kr_new, cache_ckv, cache_krt, page_table, j, scale):
    n, n_heads, kv_lora = q_lat.shape
    rope_dim = q_rope.shape[-1]
    n_pages, page = page_table.shape[1], cache_ckv.shape[2]
    per_row = lambda s1, w: pl.BlockSpec((None, s1, w), lambda b, pt: (b, 0, 0))
    hbm = pl.BlockSpec(memory_space=pl.ANY)
    grid_spec = pltpu.PrefetchScalarGridSpec(
        num_scalar_prefetch=1,
        grid=(n,),
        in_specs=[per_row(n_heads, kv_lora), per_row(n_heads, rope_dim), per_row(1, kv_lora), per_row(1, rope_dim),
                  hbm, hbm],
        out_specs=per_row(n_heads, kv_lora),
        scratch_shapes=[pltpu.VMEM((2, n_pages, page, kv_lora), cache_ckv.dtype),
                        pltpu.VMEM((2, n_pages, rope_dim, page), cache_krt.dtype),
                        pltpu.SemaphoreType.DMA((2, 2))],
    )
    return pl.pallas_call(
        functools.partial(_sample_attn_kernel, scale=scale, layer=j, n_pages=n_pages),
        grid_spec=grid_spec,
        out_shape=jax.ShapeDtypeStruct((n, n_heads, kv_lora), BF16),
        compiler_params=_params("arbitrary"),
        name="sample_attn",
    )(page_table.reshape(-1), q_lat, q_rope, c_new, kr_new, cache_ckv, cache_krt)


def _proj_ln_kernel(a_ref, x_ref, w_ref, g_ref, b_ref, o_ref, *, alpha):
    y = alpha * x_ref[...] + _dot(a_ref[...], w_ref[...])
    o_ref[...] = _layer_norm(y, g_ref[...], b_ref[...])


def _proj_ln(a, x, w, g, b, j, layer, alpha, tm_pref=512):
    m, d = x.shape
    k = a.shape[1]
    tm = _tile(m, tm_pref)
    return pl.pallas_call(
        functools.partial(_proj_ln_kernel, alpha=alpha),
        grid=(m // tm,),
        in_specs=[
            pl.BlockSpec((tm, k), lambda i: (i, 0)),
            pl.BlockSpec((tm, d), lambda i: (i, 0)),
            _resident((None, k, d), lambda i: (j, 0, 0)),
            pl.BlockSpec((None, None, 1, d), lambda i: (layer, 1, 0, 0)),
            pl.BlockSpec((None, None, 1, d), lambda i: (layer, 1, 0, 0)),
        ],
        out_specs=pl.BlockSpec((tm, d), lambda i: (i, 0)),
        out_shape=jax.ShapeDtypeStruct((m, d), F32),
        compiler_params=_params("parallel"),
        name="proj_ln",
    )(a, x, w, g, b)


def _rope_tables(pos, rope_dim):
    inv = ROPE_THETA ** (-jnp.arange(0, rope_dim, 2, dtype=F32) / rope_dim)
    ang = pos.astype(F32)[:, None] * inv[None, :]
    cos, sin = jnp.cos(ang), jnp.sin(ang)
    reps = LANES // rope_dim
    return jnp.tile(jnp.concatenate([cos, cos], -1), (1, reps)), jnp.tile(jnp.concatenate([-sin, sin], -1), (1, reps))


def _swap_halves(w):
    half = w.shape[-1] // 2
    return jnp.concatenate([w[..., half:], w[..., :half]], axis=-1)


def _pad_rows(a, rows):
    return jnp.pad(a, ((0, rows - a.shape[0]), (0, 0)))


def kernel(x_prompt, x_sample, state_conv, cache_kv_latent, cache_k_rope, page_table, meta_tokens, ln_g, ln_b, w_ffn_gate, w_ffn_up, w_ffn_down, w_conv_in, w_conv, w_conv_out, w_q_a, g_q_a, w_q_b, w_kv_a, g_kv_a, w_uk, w_uv, w_o):
    nb, seq, d = x_prompt.shape
    ns, dec_seq, _ = x_sample.shape
    assert dec_seq == 1, "the sample path handles one new token per sequence"
    n_meta = meta_tokens.shape[0]
    depth = ln_g.shape[0]
    alpha = float((2 * depth) ** 0.25)
    n_heads = w_q_b.shape[2]
    rope_dim = cache_k_rope.shape[-1]
    nope_dim = w_q_b.shape[3] - rope_dim
    kv_lora = w_uk.shape[1]
    scale = float((nope_dim + rope_dim) ** -0.5)
    past = page_table.shape[1] * cache_kv_latent.shape[2]
    assert nope_dim == LANES and w_uv.shape[-1] == LANES and n_heads % 2 == 0 and n_meta <= LANES

    xm = x_prompt.reshape(nb * seq, d)
    xe = jnp.concatenate([x_sample.reshape(ns, d), meta_tokens.astype(x_prompt.dtype)], axis=0)
    n_ext = ns + n_meta

    wqa = w_q_a.astype(BF16)
    wqn = w_q_b[..., :nope_dim].reshape(w_q_b.shape[0], w_q_b.shape[1], n_heads * nope_dim).astype(BF16)
    wq_rope = w_q_b[..., nope_dim:]
    wqr = wq_rope.reshape(w_q_b.shape[0], w_q_b.shape[1], n_heads * rope_dim).astype(BF16)
    wqrs = _swap_halves(wq_rope).reshape(wqr.shape).astype(BF16)
    wk_rope = w_kv_a[..., kv_lora:]
    zpad = jnp.zeros(wk_rope.shape[:-1] + (LANES - rope_dim,), w_kv_a.dtype)
    wkv = jnp.concatenate([w_kv_a[..., :kv_lora], wk_rope, zpad, _swap_halves(wk_rope), zpad], axis=-1).astype(BF16)
    wuk = w_uk.reshape(w_uk.shape[0], kv_lora, n_heads * nope_dim).astype(BF16)
    wuv = w_uv.reshape(w_uv.shape[0], kv_lora, -1).astype(BF16)
    wo = w_o.astype(BF16)
    ln_g4, ln_b4 = ln_g[:, :, None, :], ln_b[:, :, None, :]
    g_q3, g_kv3 = g_q_a[:, None, :], g_kv_a[:, None, :]
    cache_krt = jnp.swapaxes(cache_k_rope, 2, 3)

    cos_m, sin_m = _rope_tables(jnp.tile(n_meta + jnp.arange(seq), nb), rope_dim)
    cos_e, sin_e = _rope_tables(jnp.concatenate([jnp.full((ns,), past), jnp.arange(n_meta)]), rope_dim)

    ffn_w4 = (w_ffn_gate, w_ffn_up, w_ffn_down)
    piggyback = _can_cast_next(nb * seq, d, w_ffn_gate.shape[-1])
    ready = {}

    def ffn_both(xm, xe, layer, half, ln_idx):
        if (layer, half) in ready:
            w = ready.pop((layer, half))
            xe = _ffn(xe, *w, ln_g4, ln_b4, layer, ln_idx, alpha)
        else:
            xe, *w = _ffn_cast(xe, *ffn_w4, ln_g4, ln_b4, layer, half, ln_idx, alpha)
        nxt = (layer, 1) if half == 0 else (layer + 1, 0)
        if piggyback and nxt[0] < depth:
            xm, *w_next = _ffn(xm, *w, ln_g4, ln_b4, layer, ln_idx, alpha, cast_next=ffn_w4 + nxt)
            ready[nxt] = w_next
        else:
            xm = _ffn(xm, *w, ln_g4, ln_b4, layer, ln_idx, alpha)
        return xm, xe

    conv_p, conv_s, ckv_p, ckv_s, kr_p, kr_s = [], [], [], [], [], []
    for i in range(depth):
        j = i // N_MIXERS
        xm, xe = ffn_both(xm, xe, i, 0, 0)
        if i % N_MIXERS == 0:
            dc = w_conv_out.shape[1]
            st = state_conv[j]
            zeros = jnp.zeros((n_meta, dc), F32)
            p1 = jnp.concatenate([st[:, 1], zeros], axis=0)
            p2 = jnp.concatenate([st[:, 0], zeros], axis=0)
            rows = jnp.arange(n_ext)[:, None]
            f1 = (rows <= ns).astype(F32)
            f2 = (rows <= ns + 1).astype(F32)
            xe, u_e, wb, wc, wh, wout = _conv_mixer_rows(xe, p1, p2, f1, f2, w_conv_in, w_conv, w_conv_out,
                                                         ln_g4, ln_b4, j, i, alpha)
            prev = jnp.broadcast_to(jnp.pad(u_e[n_ext - 2:], ((SUBLANES - 2, 0), (0, 0)))[None], (nb, SUBLANES, dc))
            xm, u_last = _conv_mixer_seq(xm, prev, wb, wc, wh, w_conv, wout, ln_g4, ln_b4, j, i, alpha, seq)
            tiles_per_seq = u_last.shape[0] // nb
            conv_p.append(u_last[tiles_per_seq - 1::tiles_per_seq, SUBLANES - 2:])
            conv_s.append(jnp.stack([st[:, 1], u_e[:ns]], axis=1))
        else:
            proj = functools.partial(_mla_proj, wqa=wqa, gqa=g_q3, wqn=wqn, wqr=wqr, wqrs=wqrs, wkv=wkv, gkv=g_kv3,
                                     wuk=wuk, wuv=wuv, j=j, n_heads=n_heads)
            qc_e, kc_e, v_e, ckv_e, kr_e = proj(xe, cos_e, sin_e)
            qc_m, kc_m, v_m, ckv_m, kr_m = proj(xm, cos_m, sin_m)
            o_meta = _prefix_attention(qc_e[ns:], kc_e[ns:], v_e[ns:], n_heads, scale)
            o_main = _attention(qc_m, kc_m, v_m, _pad_rows(kc_e[ns:], LANES), _pad_rows(v_e[ns:], LANES), n_meta,
                                nb, seq, n_heads, scale)
            q_lat = _absorb_q(qc_e, wuk, j, ns, n_heads).transpose(1, 0, 2)
            q_rope_s = qc_e[:ns].reshape(ns, n_heads, 2 * LANES)[:, :, LANES:LANES + rope_dim]
            o_lat = _sample_attention(q_lat, q_rope_s, ckv_e[:ns, None], kr_e[:ns, None],
                                      cache_kv_latent, cache_krt, page_table, j, scale)
            o_s = _expand_o(o_lat.transpose(1, 0, 2), wuv, j, n_heads)
            o_e = jnp.concatenate([o_s, o_meta], axis=0)
            xe = _proj_ln(o_e, xe, wo, ln_g4, ln_b4, j, i, alpha)
            xm = _proj_ln(o_main, xm, wo, ln_g4, ln_b4, j, i, alpha)
            meta_rows = lambda a: jnp.broadcast_to(a[ns:][None], (nb, n_meta, a.shape[-1]))
            ckv_p.append(jnp.concatenate([meta_rows(ckv_e), ckv_m.reshape(nb, seq, kv_lora)], axis=1))
            kr_p.append(jnp.concatenate([meta_rows(kr_e), kr_m.reshape(nb, seq, rope_dim)], axis=1))
            ckv_s.append(ckv_e[:ns, None])
            kr_s.append(kr_e[:ns, None])
        xm, xe = ffn_both(xm, xe, i, 1, 2)

    y_prompt = xm.reshape(nb, seq, d)
    y_sample = xe[:ns].reshape(ns, dec_seq, d)
    return (y_prompt, y_sample, jnp.stack(conv_p), jnp.stack(conv_s), jnp.stack(ckv_p), jnp.stack(ckv_s),
            jnp.stack(kr_p), jnp.stack(kr_s))
```

```python
import functools

import jax
import jax.numpy as jnp
from jax import lax
from jax.experimental import pallas as pl
from jax.experimental.pallas import tpu as pltpu

LN_EPS = 1e-5
RMS_EPS = 1e-6
ROPE_THETA = 10000.0
N_MIXERS = 2
LOG2_E = 1.4426950408889634
FFN_SUB = 256
CONV_TN = 512

LANES = 128
SUBLANES = 8
VMEM_LIMIT_BYTES = 56 * 1024 * 1024

BF16 = jnp.bfloat16
F32 = jnp.float32


def _tile(n, pref):
    if n <= pref:
        return n
    for t in range(pref, 15, -1):
        if n % t == 0 and t % 16 == 0:
            return t
    return n


def _params(*sem):
    return pltpu.CompilerParams(dimension_semantics=sem, vmem_limit_bytes=VMEM_LIMIT_BYTES)


def _resident(shape, index_map):
    return pl.BlockSpec(shape, index_map, pipeline_mode=pl.Buffered(1))


def _layer_norm(y, g, b):
    mu = jnp.mean(y, axis=-1, keepdims=True)
    yc = y - mu
    var = jnp.mean(yc * yc, axis=-1, keepdims=True)
    return yc * lax.rsqrt(var + LN_EPS) * g + b


def _rms_norm(y, g):
    return y * lax.rsqrt(jnp.mean(y * y, axis=-1, keepdims=True) + RMS_EPS) * g


def _dot(a, b):
    return jnp.dot(a, b, preferred_element_type=F32)


def _dot_nt(a, b):
    return lax.dot_general(a, b, (((1,), (1,)), ((), ())), preferred_element_type=F32)


def _ffn_chunk(xb, wg_ref, wu_ref, wd_ref):
    tf = wg_ref.shape[1]
    sub = FFN_SUB if tf % FFN_SUB == 0 else tf
    part = None
    for c in range(tf // sub):
        cols = slice(c * sub, (c + 1) * sub)
        gate = _dot(xb, wg_ref[:, cols])
        up = _dot(xb, wu_ref[:, cols])
        h = gate / (1.0 + jnp.exp(-gate)) * up
        d = _dot(h.astype(BF16), wd_ref[cols, :])
        part = d if part is None else part + d
    return part


def _ffn_step(f, n_f, x_ref, wg_ref, wu_ref, wd_ref, g_ref, b_ref, o_ref, xb_ref, acc_ref, alpha):
    @pl.when(f == 0)
    def _():
        xb_ref[...] = x_ref[...].astype(BF16)
        acc_ref[...] = jnp.zeros_like(acc_ref)

    acc_ref[...] += _ffn_chunk(xb_ref[...], wg_ref, wu_ref, wd_ref)

    @pl.when(f == n_f - 1)
    def _():
        y = alpha * x_ref[...] + 0.5 * acc_ref[...]
        o_ref[...] = _layer_norm(y, g_ref[...], b_ref[...])


def _ffn_deferred_ln_body(x_ref, wg_ref, wu_ref, wd_ref, g_ref, b_ref, o_ref, xb_ref, acc_ref, y_ref,
                          *, alpha, n_i, n_f):
    s = pl.program_id(0)
    i, f = s // n_f, s % n_f

    @pl.when(s == 0)
    def _():
        y_ref[...] = jnp.zeros_like(y_ref)

    @pl.when(jnp.logical_and(f == 0, i < n_i))
    def _():
        o_ref[...] = _layer_norm(y_ref[...], g_ref[...], b_ref[...])
        xb = x_ref[...].astype(BF16)
        xb_ref[...] = xb
        part = _ffn_chunk(xb, wg_ref, wu_ref, wd_ref)
        if n_f == 1:
            y_ref[...] = alpha * x_ref[...] + 0.5 * part
        else:
            acc_ref[...] = part

    @pl.when(i == n_i)
    def _():
        o_ref[...] = _layer_norm(y_ref[...], g_ref[...], b_ref[...])

    @pl.when(jnp.logical_and(f != 0, f < n_f - 1))
    def _():
        acc_ref[...] += _ffn_chunk(xb_ref[...], wg_ref, wu_ref, wd_ref)

    @pl.when(jnp.logical_and(f != 0, f == n_f - 1))
    def _():
        acc = acc_ref[...] + _ffn_chunk(xb_ref[...], wg_ref, wu_ref, wd_ref)
        y_ref[...] = alpha * x_ref[...] + 0.5 * acc


def _ffn_kernel(x_ref, wg_ref, wu_ref, wd_ref, g_ref, b_ref, o_ref, xb_ref, acc_ref, y_ref, **kw):
    _ffn_deferred_ln_body(x_ref, wg_ref, wu_ref, wd_ref, g_ref, b_ref, o_ref, xb_ref, acc_ref, y_ref, **kw)


def _ffn_cast_kernel(x_ref, wg_ref, wu_ref, wd_ref, g_ref, b_ref, o_ref, wgb_ref, wub_ref, wdb_ref,
                     xb_ref, acc_ref, *, alpha):
    wgb_ref[...] = wg_ref[...].astype(BF16)
    wub_ref[...] = wu_ref[...].astype(BF16)
    wdb_ref[...] = wd_ref[...].astype(BF16)
    _ffn_step(pl.program_id(0), pl.num_programs(0), x_ref, wgb_ref, wub_ref, wdb_ref,
              g_ref, b_ref, o_ref, xb_ref, acc_ref, alpha)


def _ffn_cast_next_kernel(x_ref, wg_ref, wu_ref, wd_ref, g_ref, b_ref, ng_ref, nu_ref, nd_ref,
                          o_ref, ngb_ref, nub_ref, ndb_ref, xb_ref, acc_ref, y_ref, **kw):
    ngb_ref[...] = ng_ref[...].astype(BF16)
    nub_ref[...] = nu_ref[...].astype(BF16)
    ndb_ref[...] = nd_ref[...].astype(BF16)
    _ffn_deferred_ln_body(x_ref, wg_ref, wu_ref, wd_ref, g_ref, b_ref, o_ref, xb_ref, acc_ref, y_ref, **kw)


def _ffn_tiles(m, d, ff, tm_pref=512, tf_pref=512):
    return _tile(m, tm_pref), _tile(ff, tf_pref)


def _can_cast_next(m, d, ff):
    tm, _ = _ffn_tiles(m, d, ff)
    n_i = m // tm
    return d % n_i == 0 and (d // n_i) % LANES == 0


def _ffn(x, wg, wu, wd, g, b, layer, ln_idx, alpha, cast_next=None):
    m, d = x.shape
    ff = wd.shape[0]
    tm, tf = _ffn_tiles(m, d, ff)
    n_i, n_f = m // tm, ff // tf
    assert wg.shape == (n_f, d, tf)
    last = n_i * n_f
    row = lambda s: jnp.minimum(s // n_f, n_i - 1)
    chunk = lambda s: jnp.where(s == last, n_f - 1, s % n_f)
    prev_row = lambda s: jnp.maximum(s // n_f - 1, 0)
    in_specs = [
        pl.BlockSpec((tm, d), lambda s: (row(s), 0)),
        pl.BlockSpec((None, d, tf), lambda s: (chunk(s), 0, 0)),
        pl.BlockSpec((None, d, tf), lambda s: (chunk(s), 0, 0)),
        pl.BlockSpec((tf, d), lambda s: (chunk(s), 0)),
        pl.BlockSpec((None, None, 1, d), lambda s: (layer, ln_idx, 0, 0)),
        pl.BlockSpec((None, None, 1, d), lambda s: (layer, ln_idx, 0, 0)),
    ]
    out_spec = pl.BlockSpec((tm, d), lambda s: (prev_row(s), 0))
    out_shape = jax.ShapeDtypeStruct((m, d), F32)
    scratch = [pltpu.VMEM((tm, d), BF16), pltpu.VMEM((tm, d), F32), pltpu.VMEM((tm, d), F32)]
    kw = dict(alpha=alpha, n_i=n_i, n_f=n_f)
    if cast_next is None:
        return pl.pallas_call(
            functools.partial(_ffn_kernel, **kw),
            grid=(last + 1,),
            in_specs=in_specs,
            out_specs=out_spec,
            out_shape=out_shape,
            scratch_shapes=scratch,
            compiler_params=_params("arbitrary"),
            name="ffn_macaron",
        )(x, wg, wu, wd, g, b)
    ng, nu, nd, layer_n, half_n = cast_next
    td = d // n_i
    return pl.pallas_call(
        functools.partial(_ffn_cast_next_kernel, **kw),
        grid=(last + 1,),
        in_specs=in_specs + [
            pl.BlockSpec((None, None, td, tf), lambda s: (layer_n, half_n, row(s), chunk(s))),
            pl.BlockSpec((None, None, td, tf), lambda s: (layer_n, half_n, row(s), chunk(s))),
            pl.BlockSpec((None, None, tf, td), lambda s: (layer_n, half_n, chunk(s), row(s))),
        ],
        out_specs=[out_spec,
                   pl.BlockSpec((None, td, tf), lambda s: (chunk(s), row(s), 0)),
                   pl.BlockSpec((None, td, tf), lambda s: (chunk(s), row(s), 0)),
                   pl.BlockSpec((tf, td), lambda s: (chunk(s), row(s)))],
        out_shape=[out_shape, jax.ShapeDtypeStruct((n_f, d, tf), BF16), jax.ShapeDtypeStruct((n_f, d, tf), BF16),
                   jax.ShapeDtypeStruct((ff, d), BF16)],
        scratch_shapes=scratch,
        compiler_params=_params("arbitrary"),
        name="ffn_macaron_cast_next",
    )(x, wg, wu, wd, g, b, ng, nu, nd)


def _ffn_cast(x, wg, wu, wd, g, b, layer, half, ln_idx, alpha, tf_pref=256):
    m, d = x.shape
    ff = wg.shape[-1]
    tf = _tile(ff, tf_pref)
    _, tf_main = _ffn_tiles(m, d, ff)
    assert tf_main % tf == 0
    per = tf_main // tf
    wide_out = pl.BlockSpec((None, d, tf), lambda f: (f // per, 0, f % per))
    wide_shape = jax.ShapeDtypeStruct((ff // tf_main, d, tf_main), BF16)
    return pl.pallas_call(
        functools.partial(_ffn_cast_kernel, alpha=alpha),
        grid=(ff // tf,),
        in_specs=[
            pl.BlockSpec((m, d), lambda f: (0, 0)),
            pl.BlockSpec((None, None, d, tf), lambda f: (layer, half, 0, f)),
            pl.BlockSpec((None, None, d, tf), lambda f: (layer, half, 0, f)),
            pl.BlockSpec((None, None, tf, d), lambda f: (layer, half, f, 0)),
            pl.BlockSpec((None, None, 1, d), lambda f: (layer, ln_idx, 0, 0)),
            pl.BlockSpec((None, None, 1, d), lambda f: (layer, ln_idx, 0, 0)),
        ],
        out_specs=[
            pl.BlockSpec((m, d), lambda f: (0, 0)),
            wide_out,
            wide_out,
            pl.BlockSpec((tf, d), lambda f: (f, 0)),
        ],
        out_shape=[jax.ShapeDtypeStruct((m, d), F32), wide_shape, wide_shape,
                   jax.ShapeDtypeStruct((ff, d), BF16)],
        scratch_shapes=[pltpu.VMEM((m, d), BF16), pltpu.VMEM((m, d), F32)],
        compiler_params=_params("arbitrary"),
        name="ffn_macaron_cast",
    )(x, wg, wu, wd, g, b)


def _conv_taps(u, u_m1, u_m2, wconv_ref):
    w = wconv_ref[...]
    return w[0:1] * u_m2 + w[1:2] * u_m1 + w[2:3] * u


def _conv_seq_kernel(x_ref, prev_ref, wb_ref, wc_ref, wh_ref, wconv_ref, wout_ref, g_ref, b_ref,
                     o_ref, ulast_ref, xb_ref, acc_ref, carry_ref, *, alpha, tiles_per_seq):
    i, c = pl.program_id(0), pl.program_id(1)
    tm = x_ref.shape[0]

    @pl.when(c == 0)
    def _():
        xb_ref[...] = x_ref[...].astype(BF16)
        acc_ref[...] = jnp.zeros_like(acc_ref)

    @pl.when(i == 0)
    def _():
        carry_ref[c] = jnp.zeros(carry_ref.shape[1:], F32)

    xb = xb_ref[...]
    u = _dot(xb, wc_ref[...]) * _dot(xb, wh_ref[...])
    seq_start = (i % tiles_per_seq) == 0
    tail = jnp.where(seq_start, prev_ref[...], carry_ref[c])
    row = lax.broadcasted_iota(jnp.int32, u.shape, 0)
    u_m1 = jnp.where(row == 0, tail[7:8], pltpu.roll(u, 1, 0))
    u_m2 = jnp.where(row == 0, tail[6:7], jnp.where(row == 1, tail[7:8], pltpu.roll(u, 2, 0)))
    conv = _conv_taps(u, u_m1, u_m2, wconv_ref)
    last = u[tm - SUBLANES:, :]
    carry_ref[c] = last
    ulast_ref[...] = last
    z = _dot(xb, wb_ref[...]) * conv
    acc_ref[...] += _dot(z.astype(BF16), wout_ref[...])

    @pl.when(c == pl.num_programs(1) - 1)
    def _():
        o_ref[...] = _layer_norm(alpha * x_ref[...] + acc_ref[...], g_ref[...], b_ref[...])


def _conv_rows_kernel(x_ref, p1_ref, p2_ref, f1_ref, f2_ref, wb_ref, wc_ref, wh_ref, wconv_ref, wout_ref,
                      g_ref, b_ref, o_ref, u_ref, wbb_ref, wcb_ref, whb_ref, woutb_ref, xb_ref, acc_ref, *, alpha):
    c = pl.program_id(0)

    @pl.when(c == 0)
    def _():
        xb_ref[...] = x_ref[...].astype(BF16)
        acc_ref[...] = jnp.zeros_like(acc_ref)

    wb, wc, wh = wb_ref[...].astype(BF16), wc_ref[...].astype(BF16), wh_ref[...].astype(BF16)
    wout = wout_ref[...].astype(BF16)
    wbb_ref[...] = wb
    wcb_ref[...] = wc
    whb_ref[...] = wh
    woutb_ref[...] = wout

    xb = xb_ref[...]
    u = _dot(xb, wc) * _dot(xb, wh)
    u_m1 = jnp.where(f1_ref[...] > 0.5, p1_ref[...], pltpu.roll(u, 1, 0))
    u_m2 = jnp.where(f2_ref[...] > 0.5, p2_ref[...], pltpu.roll(u, 2, 0))
    conv = _conv_taps(u, u_m1, u_m2, wconv_ref)
    u_ref[...] = u
    z = _dot(xb, wb) * conv
    acc_ref[...] += _dot(z.astype(BF16), wout)

    @pl.when(c == pl.num_programs(0) - 1)
    def _():
        o_ref[...] = _layer_norm(alpha * x_ref[...] + acc_ref[...], g_ref[...], b_ref[...])


def _conv_mixer_seq(x, prev, wb, wc, wh, w_conv, w_out, g, b, j, layer, alpha, seq_len, tm_pref=512):
    m, d = x.shape
    dc = w_out.shape[0]
    tm, tn = _tile(seq_len, tm_pref), _tile(dc, CONV_TN)
    tiles_per_seq = seq_len // tm
    n_c = dc // tn
    assert wb.shape == (n_c, d, tn)
    w_spec = pl.BlockSpec((None, d, tn), lambda i, c: (c, 0, 0))
    return pl.pallas_call(
        functools.partial(_conv_seq_kernel, alpha=alpha, tiles_per_seq=tiles_per_seq),
        grid=(m // tm, n_c),
        in_specs=[
            pl.BlockSpec((tm, d), lambda i, c: (i, 0)),
            pl.BlockSpec((None, SUBLANES, tn), lambda i, c: (i // tiles_per_seq, 0, c)),
            w_spec, w_spec, w_spec,
            pl.BlockSpec((None, 3, tn), lambda i, c: (j, 0, c)),
            pl.BlockSpec((tn, d), lambda i, c: (c, 0)),
            pl.BlockSpec((None, None, 1, d), lambda i, c: (layer, 1, 0, 0)),
            pl.BlockSpec((None, None, 1, d), lambda i, c: (layer, 1, 0, 0)),
        ],
        out_specs=[
            pl.BlockSpec((tm, d), lambda i, c: (i, 0)),
            pl.BlockSpec((None, SUBLANES, tn), lambda i, c: (i, 0, c)),
        ],
        out_shape=[jax.ShapeDtypeStruct((m, d), F32),
                   jax.ShapeDtypeStruct((m // tm, SUBLANES, dc), F32)],
        scratch_shapes=[pltpu.VMEM((tm, d), BF16), pltpu.VMEM((tm, d), F32),
                        pltpu.VMEM((n_c, SUBLANES, tn), F32)],
        compiler_params=_params("arbitrary", "arbitrary"),
        name="conv_mixer_seq",
    )(x, prev, wb, wc, wh, w_conv, w_out, g, b)


def _conv_mixer_rows(x, p1, p2, f1, f2, w_in, w_conv, w_out, g, b, j, layer, alpha, tn_pref=256):
    m, d = x.shape
    dc = w_out.shape[1]
    tn = _tile(dc, tn_pref)
    tn_main = _tile(dc, CONV_TN)
    assert tn_main % tn == 0
    per = tn_main // tn
    n_c = dc // tn
    w_spec = lambda part: pl.BlockSpec((None, d, tn), lambda c: (j, 0, part * n_c + c))
    wb_out = pl.BlockSpec((None, d, tn), lambda c: (c // per, 0, c % per))
    wide_shape = jax.ShapeDtypeStruct((dc // tn_main, d, tn_main), BF16)
    return pl.pallas_call(
        functools.partial(_conv_rows_kernel, alpha=alpha),
        grid=(n_c,),
        in_specs=[
            pl.BlockSpec((m, d), lambda c: (0, 0)),
            pl.BlockSpec((m, tn), lambda c: (0, c)),
            pl.BlockSpec((m, tn), lambda c: (0, c)),
            pl.BlockSpec((m, 1), lambda c: (0, 0)),
            pl.BlockSpec((m, 1), lambda c: (0, 0)),
            w_spec(0), w_spec(1), w_spec(2),
            pl.BlockSpec((None, 3, tn), lambda c: (j, 0, c)),
            pl.BlockSpec((None, tn, d), lambda c: (j, c, 0)),
            pl.BlockSpec((None, None, 1, d), lambda c: (layer, 1, 0, 0)),
            pl.BlockSpec((None, None, 1, d), lambda c: (layer, 1, 0, 0)),
        ],
        out_specs=[
            pl.BlockSpec((m, d), lambda c: (0, 0)),
            pl.BlockSpec((m, tn), lambda c: (0, c)),
            wb_out, wb_out, wb_out,
            pl.BlockSpec((tn, d), lambda c: (c, 0)),
        ],
        out_shape=[jax.ShapeDtypeStruct((m, d), F32), jax.ShapeDtypeStruct((m, dc), F32),
                   wide_shape, wide_shape, wide_shape, jax.ShapeDtypeStruct((dc, d), BF16)],
        scratch_shapes=[pltpu.VMEM((m, d), BF16), pltpu.VMEM((m, d), F32)],
        compiler_params=_params("arbitrary"),
        name="conv_mixer_rows",
    )(x, p1, p2, f1, f2, w_in, w_in, w_in, w_conv, w_out, g, b)


def _mla_proj_kernel(x_ref, cos_ref, sin_ref, wqa_ref, gqa_ref, wqn_ref, wqr_ref, wqrs_ref, wkv_ref, gkv_ref,
                     wuk_ref, wuv_ref, qc_ref, kc_ref, v_ref, ckv_ref, kr_ref, *, n_heads, kv_lora, rope_dim):
    half = LANES // 2
    assert rope_dim == half, "rope halves are packed two heads per 128 lanes"
    xb = x_ref[...].astype(BF16)
    cos, sin = cos_ref[...], sin_ref[...]
    lane = lax.broadcasted_iota(jnp.int32, cos.shape, 1)
    low = lane < half

    cq = _rms_norm(_dot(xb, wqa_ref[...]), gqa_ref[...]).astype(BF16)
    qn = _dot(cq, wqn_ref[...])
    qr_a = _dot(cq, wqr_ref[...])
    qr_b = _dot(cq, wqrs_ref[...])
    for pair in range(n_heads // 2):
        sl = slice(pair * LANES, (pair + 1) * LANES)
        both = qr_a[:, sl] * cos + qr_b[:, sl] * sin
        for k in range(2):
            h = 2 * pair + k
            part = both if k == 0 else pltpu.roll(both, half, 1)
            qc_ref[:, h * 2 * LANES:h * 2 * LANES + LANES] = qn[:, h * LANES:(h + 1) * LANES].astype(BF16)
            qc_ref[:, h * 2 * LANES + LANES:(h + 1) * 2 * LANES] = jnp.where(low, part, 0.0).astype(BF16)

    kv = _dot(xb, wkv_ref[...])
    ckv = _rms_norm(kv[:, :kv_lora], gkv_ref[...])
    ckv_ref[...] = ckv
    kr = kv[:, kv_lora:kv_lora + LANES] * cos + kv[:, kv_lora + LANES:] * sin
    kr = jnp.where(low, kr, 0.0)
    kr_ref[...] = kr[:, :rope_dim]
    ckv_b = ckv.astype(BF16)
    kn = _dot(ckv_b, wuk_ref[...])
    v = _dot(ckv_b, wuv_ref[...])
    kr_b = kr.astype(BF16)
    ones = jnp.ones((xb.shape[0], LANES), BF16)
    for h in range(n_heads):
        kc_ref[:, h * 2 * LANES:h * 2 * LANES + LANES] = kn[:, h * LANES:(h + 1) * LANES].astype(BF16)
        kc_ref[:, h * 2 * LANES + LANES:(h + 1) * 2 * LANES] = kr_b
        v_ref[:, h * 2 * LANES:h * 2 * LANES + LANES] = v[:, h * LANES:(h + 1) * LANES].astype(BF16)
        v_ref[:, h * 2 * LANES + LANES:(h + 1) * 2 * LANES] = ones


def _mla_proj(x, cos, sin, wqa, gqa, wqn, wqr, wqrs, wkv, gkv, wuk, wuv, j, n_heads, tm_pref=256):
    m, d = x.shape
    kv_lora = wuk.shape[1]
    rope_dim = wqr.shape[-1] // n_heads
    tm = _tile(m, tm_pref)
    hq = n_heads * 2 * LANES
    hv = hq
    row = lambda w: pl.BlockSpec((tm, w), lambda i: (i, 0))
    full = lambda a: _resident((None,) + a.shape[1:], lambda i: (j,) + (0,) * (a.ndim - 1))
    return pl.pallas_call(
        functools.partial(_mla_proj_kernel, n_heads=n_heads, kv_lora=kv_lora, rope_dim=rope_dim),
        grid=(m // tm,),
        in_specs=[row(d), row(LANES), row(LANES), full(wqa), full(gqa), full(wqn), full(wqr), full(wqrs),
                  full(wkv), full(gkv), full(wuk), full(wuv)],
        out_specs=[row(hq), row(hq), row(hv), row(kv_lora), row(rope_dim)],
        out_shape=[jax.ShapeDtypeStruct((m, hq), BF16), jax.ShapeDtypeStruct((m, hq), BF16),
                   jax.ShapeDtypeStruct((m, hv), BF16), jax.ShapeDtypeStruct((m, kv_lora), F32),
                   jax.ShapeDtypeStruct((m, rope_dim), F32)],
        compiler_params=_params("parallel"),
        name="mla_proj",
    )(x, cos, sin, wqa, gqa, wqn, wqr, wqrs, wkv, gkv, wuk, wuv)


def _lane_chunks(s):
    return [s[:, c * LANES:(c + 1) * LANES] for c in range(s.shape[1] // LANES)]


def _row_max(parts):
    m = functools.reduce(jnp.maximum, parts)
    return jnp.broadcast_to(jnp.max(m, axis=1, keepdims=True), m.shape)


def _attn_kernel(q_ref, k_ref, v_ref, kp_ref, vp_ref, o_ref, m_ref, acc_ref, *, c_exp, tk, n_pre):
    qi = pl.program_id(2)
    dq = 2 * LANES
    n_grp = q_ref.shape[1] // dq
    assert q_ref.shape[0] == tk and v_ref.shape[1] == n_grp * dq
    qs = [q_ref[:, h * dq:(h + 1) * dq] for h in range(n_grp)]
    hq = lambda h: slice(h * dq, (h + 1) * dq)
    k_blk = lambda h, start: k_ref[pl.ds(start, tk), hq(h)]
    v_blk = lambda h, start: v_ref[pl.ds(start, tk), hq(h)]
    hl = lambda h: slice(h * LANES, (h + 1) * LANES)

    start = pl.multiple_of(qi * tk, tk)
    row = lax.broadcasted_iota(jnp.int32, (tk, tk), 0)
    col = lax.broadcasted_iota(jnp.int32, (tk, tk), 1)
    causal = col <= row
    real_pre = lax.broadcasted_iota(jnp.int32, (tk, LANES), 1) < n_pre
    for h in range(n_grp):
        s_d = jnp.where(causal, _dot_nt(qs[h], k_blk(h, start)), -jnp.inf)
        s_p = jnp.where(real_pre, _dot_nt(qs[h], kp_ref[:, hq(h)]), -jnp.inf)
        parts = _lane_chunks(s_d) + [s_p]
        m = _row_max(parts)
        p_parts = [jnp.exp2((x - m) * c_exp) for x in parts]
        m_ref[:, hl(h)] = m
        acc_ref[:, hq(h)] = (_dot(jnp.concatenate(p_parts[:-1], axis=1).astype(BF16), v_blk(h, start))
                             + _dot(p_parts[-1].astype(BF16), vp_ref[:, hq(h)]))

    def block(jb):
        start = pl.multiple_of(jb * tk, tk)
        for h in range(n_grp):
            parts = _lane_chunks(_dot_nt(qs[h], k_blk(h, start)))
            m_old = m_ref[:, hl(h)]
            m_new = jnp.maximum(m_old, _row_max(parts))
            a = jnp.exp2((m_old - m_new) * c_exp)
            p = jnp.concatenate([jnp.exp2((x - m_new) * c_exp) for x in parts], axis=1).astype(BF16)
            acc_ref[:, hq(h)] = jnp.concatenate([a, a], axis=1) * acc_ref[:, hq(h)] + _dot(p, v_blk(h, start))
            m_ref[:, hl(h)] = m_new

    def pair(jp, carry):
        block(2 * jp)
        block(2 * jp + 1)
        return carry

    lax.fori_loop(0, qi // 2, pair, 0)

    @pl.when(qi % 2 == 1)
    def _():
        block(qi - 1)

    for h in range(n_grp):
        acc = acc_ref[:, hq(h)]
        o_ref[:, hl(h)] = (acc[:, :LANES] / acc[:, LANES:]).astype(o_ref.dtype)


def _attention(q, k, v, k_pre, v_pre, n_pre, n_seq, seq_len, n_heads, scale, t_pref=512, heads_per_step=4):
    dq = 2 * LANES
    t = _tile(seq_len, t_pref)
    n_t = seq_len // t
    g = max(x for x in range(1, heads_per_step + 1) if n_heads % x == 0)
    assert t % LANES == 0 and k_pre.shape[0] == LANES and v.shape[1] == n_heads * dq
    return pl.pallas_call(
        functools.partial(_attn_kernel, c_exp=scale * LOG2_E, tk=t, n_pre=n_pre),
        grid=(n_seq, n_heads // g, n_t),
        in_specs=[
            pl.BlockSpec((t, g * dq), lambda b, h, i: (b * n_t + i, h)),
            pl.BlockSpec((seq_len, g * dq), lambda b, h, i: (b, h)),
            pl.BlockSpec((seq_len, g * dq), lambda b, h, i: (b, h)),
            pl.BlockSpec((LANES, g * dq), lambda b, h, i: (0, h)),
            pl.BlockSpec((LANES, g * dq), lambda b, h, i: (0, h)),
        ],
        out_specs=pl.BlockSpec((t, g * LANES), lambda b, h, i: (b * n_t + i, h)),
        out_shape=jax.ShapeDtypeStruct((n_seq * seq_len, n_heads * LANES), BF16),
        scratch_shapes=[pltpu.VMEM((t, g * LANES), F32), pltpu.VMEM((t, g * dq), F32)],
        compiler_params=_params("parallel", "parallel", "arbitrary"),
        name="attn_causal",
    )(q, k, v, k_pre, v_pre)


def _prefix_attn_kernel(q_ref, k_ref, v_ref, o_ref, *, scale):
    s = _dot_nt(q_ref[...], k_ref[...]) * scale
    row = lax.broadcasted_iota(jnp.int32, s.shape, 0)
    col = lax.broadcasted_iota(jnp.int32, s.shape, 1)
    s = jnp.where(col <= row, s, -jnp.inf)
    p = jnp.exp(s - jnp.max(s, axis=1, keepdims=True))
    o = _dot(p.astype(BF16), v_ref[:, :LANES]) / jnp.sum(p, axis=1, keepdims=True)
    o_ref[...] = o.astype(o_ref.dtype)


def _prefix_attention(q, k, v, n_heads, scale):
    n = q.shape[0]
    dq = 2 * LANES
    blk = pl.BlockSpec((n, dq), lambda h: (0, h))
    return pl.pallas_call(
        functools.partial(_prefix_attn_kernel, scale=scale),
        grid=(n_heads,),
        in_specs=[blk, blk, blk],
        out_specs=pl.BlockSpec((n, LANES), lambda h: (0, h)),
        out_shape=jax.ShapeDtypeStruct((n, n_heads * LANES), BF16),
        compiler_params=_params("parallel"),
        name="attn_prefix",
    )(q, k, v)


def _absorb_q_kernel(q_ref, wuk_ref, o_ref):
    o_ref[...] = _dot_nt(q_ref[:, :LANES], wuk_ref[...]).astype(o_ref.dtype)


def _absorb_q(q_cat, wuk, j, n_rows, n_heads):
    kv_lora = wuk.shape[1]
    return pl.pallas_call(
        _absorb_q_kernel,
        grid=(n_heads,),
        in_specs=[pl.BlockSpec((n_rows, 2 * LANES), lambda h: (0, h)),
                  pl.BlockSpec((None, kv_lora, LANES), lambda h: (j, 0, h))],
        out_specs=pl.BlockSpec((None, n_rows, kv_lora), lambda h: (h, 0, 0)),
        out_shape=jax.ShapeDtypeStruct((n_heads, n_rows, kv_lora), BF16),
        compiler_params=_params("parallel"),
        name="absorb_q",
    )(q_cat, wuk)


def _expand_o_kernel(o_ref, wuv_ref, out_ref):
    out_ref[...] = _dot(o_ref[...], wuv_ref[...]).astype(out_ref.dtype)


def _expand_o(o_lat, wuv, j, n_heads):
    _, n_rows, kv_lora = o_lat.shape
    hv = wuv.shape[-1] // n_heads
    return pl.pallas_call(
        _expand_o_kernel,
        grid=(n_heads,),
        in_specs=[pl.BlockSpec((None, n_rows, kv_lora), lambda h: (h, 0, 0)),
                  pl.BlockSpec((None, kv_lora, hv), lambda h: (j, 0, h))],
        out_specs=pl.BlockSpec((n_rows, hv), lambda h: (0, h)),
        out_shape=jax.ShapeDtypeStruct((n_rows, n_heads * hv), BF16),
        compiler_params=_params("parallel"),
        name="expand_o",
    )(o_lat, wuv)


def _sample_attn_kernel(pt_ref, ql_ref, qr_ref, cnew_ref, krnew_ref, ckv_hbm, krt_hbm, o_ref,
                        ckv_buf, krt_buf, sem, *, scale, layer, n_pages):
    b = pl.program_id(0)
    slot = b % 2

    def page_copies(seq, slot, p):
        page_id = pt_ref[seq * n_pages + p]
        return (pltpu.make_async_copy(ckv_hbm.at[layer, page_id], ckv_buf.at[slot, p], sem.at[slot, 0]),
                pltpu.make_async_copy(krt_hbm.at[layer, page_id], krt_buf.at[slot, p], sem.at[slot, 1]))

    def start_gather(seq, slot):
        for p in range(n_pages):
            for cp in page_copies(seq, slot, p):
                cp.start()

    @pl.when(b == 0)
    def _():
        start_gather(0, 0)

    @pl.when(b + 1 < pl.num_programs(0))
    def _():
        start_gather(b + 1, 1 - slot)

    for p in range(n_pages):
        for cp in page_copies(b, slot, p):
            cp.wait()

    ql, qr = ql_ref[...], qr_ref[...]
    cn = cnew_ref[...].astype(BF16).astype(F32)
    kn = krnew_ref[...].astype(BF16).astype(F32)
    s_new = (jnp.sum(ql.astype(F32) * cn, axis=-1, keepdims=True)
             + jnp.sum(qr.astype(F32) * kn, axis=-1, keepdims=True)) * scale

    pages = [ckv_buf[slot, p].astype(BF16) for p in range(n_pages)]
    s = jnp.concatenate(
        [_dot_nt(ql, pages[p]) + _dot(qr, krt_buf[slot, p].astype(BF16)) for p in range(n_pages)],
        axis=1) * scale
    m = jnp.maximum(jnp.max(s, axis=-1, keepdims=True), s_new)
    e = jnp.exp(s - m)
    e_new = jnp.exp(s_new - m)
    eb = e.astype(BF16)
    page = pages[0].shape[0]
    acc = e_new.astype(BF16).astype(F32) * cn
    for p in range(n_pages):
        acc += _dot(eb[:, p * page:(p + 1) * page], pages[p])
    o_ref[...] = (acc / (jnp.sum(e, axis=-1, keepdims=True) + e_new)).astype(o_ref.dtype)


def _sample_attention(q_lat, q_rope, c_new, kr_new, cache_ckv, cache_krt, page_table, j, scale):
    n, n_heads, kv_lora = q_lat.shape
    rope_dim = q_rope.shape[-1]
    n_pages, page = page_table.shape[1], cache_ckv.shape[2]
    per_row = lambda s1, w: pl.BlockSpec((None, s1, w), lambda b, pt: (b, 0, 0))
    hbm = pl.BlockSpec(memory_space=pl.ANY)
    grid_spec = pltpu.PrefetchScalarGridSpec(
        num_scalar_prefetch=1,
        grid=(n,),
        in_specs=[per_row(n_heads, kv_lora), per_row(n_heads, rope_dim), per_row(1, kv_lora), per_row(1, rope_dim),
                  hbm, hbm],
        out_specs=per_row(n_heads, kv_lora),
        scratch_shapes=[pltpu.VMEM((2, n_pages, page, kv_lora), cache_ckv.dtype),
                        pltpu.VMEM((2, n_pages, rope_dim, page), cache_krt.dtype),
                        pltpu.SemaphoreType.DMA((2, 2))],
    )
    return pl.pallas_call(
        functools.partial(_sample_attn_kernel, scale=scale, layer=j, n_pages=n_pages),
        grid_spec=grid_spec,
        out_shape=jax.ShapeDtypeStruct((n, n_heads, kv_lora), BF16),
        compiler_params=_params("arbitrary"),
        name="sample_attn",
    )(page_table.reshape(-1), q_lat, q_rope, c_new, kr_new, cache_ckv, cache_krt)


def _proj_ln_kernel(a_ref, x_ref, w_ref, g_ref, b_ref, o_ref, *, alpha):
    y = alpha * x_ref[...] + _dot(a_ref[...], w_ref[...])
    o_ref[...] = _layer_norm(y, g_ref[...], b_ref[...])


def _proj_ln(a, x, w, g, b, j, layer, alpha, tm_pref=512):
    m, d = x.shape
    k = a.shape[1]
    tm = _tile(m, tm_pref)
    return pl.pallas_call(
        functools.partial(_proj_ln_kernel, alpha=alpha),
        grid=(m // tm,),
        in_specs=[
            pl.BlockSpec((tm, k), lambda i: (i, 0)),
            pl.BlockSpec((tm, d), lambda i: (i, 0)),
            _resident((None, k, d), lambda i: (j, 0, 0)),
            pl.BlockSpec((None, None, 1, d), lambda i: (layer, 1, 0, 0)),
            pl.BlockSpec((None, None, 1, d), lambda i: (layer, 1, 0, 0)),
        ],
        out_specs=pl.BlockSpec((tm, d), lambda i: (i, 0)),
        out_shape=jax.ShapeDtypeStruct((m, d), F32),
        compiler_params=_params("parallel"),
        name="proj_ln",
    )(a, x, w, g, b)


def _rope_tables(pos, rope_dim):
    inv = ROPE_THETA ** (-jnp.arange(0, rope_dim, 2, dtype=F32) / rope_dim)
    ang = pos.astype(F32)[:, None] * inv[None, :]
    cos, sin = jnp.cos(ang), jnp.sin(ang)
    reps = LANES // rope_dim
    return jnp.tile(jnp.concatenate([cos, cos], -1), (1, reps)), jnp.tile(jnp.concatenate([-sin, sin], -1), (1, reps))


def _swap_halves(w):
    half = w.shape[-1] // 2
    return jnp.concatenate([w[..., half:], w[..., :half]], axis=-1)


def _pad_rows(a, rows):
    return jnp.pad(a, ((0, rows - a.shape[0]), (0, 0)))


def kernel(x_prompt, x_sample, state_conv, cache_kv_latent, cache_k_rope, page_table, meta_tokens, ln_g, ln_b, w_ffn_gate, w_ffn_up, w_ffn_down, w_conv_in, w_conv, w_conv_out, w_q_a, g_q_a, w_q_b, w_kv_a, g_kv_a, w_uk, w_uv, w_o):
    nb, seq, d = x_prompt.shape
    ns, dec_seq, _ = x_sample.shape
    assert dec_seq == 1, "the sample path handles one new token per sequence"
    n_meta = meta_tokens.shape[0]
    depth = ln_g.shape[0]
    alpha = float((2 * depth) ** 0.25)
    n_heads = w_q_b.shape[2]
    rope_dim = cache_k_rope.shape[-1]
    nope_dim = w_q_b.shape[3] - rope_dim
    kv_lora = w_uk.shape[1]
    scale = float((nope_dim + rope_dim) ** -0.5)
    past = page_table.shape[1] * cache_kv_latent.shape[2]
    assert nope_dim == LANES and w_uv.shape[-1] == LANES and n_heads % 2 == 0 and n_meta <= LANES

    xm = x_prompt.reshape(nb * seq, d)
    xe = jnp.concatenate([x_sample.reshape(ns, d), meta_tokens.astype(x_prompt.dtype)], axis=0)
    n_ext = ns + n_meta

    wqa = w_q_a.astype(BF16)
    wqn = w_q_b[..., :nope_dim].reshape(w_q_b.shape[0], w_q_b.shape[1], n_heads * nope_dim).astype(BF16)
    wq_rope = w_q_b[..., nope_dim:]
    wqr = wq_rope.reshape(w_q_b.shape[0], w_q_b.shape[1], n_heads * rope_dim).astype(BF16)
    wqrs = _swap_halves(wq_rope).reshape(wqr.shape).astype(BF16)
    wk_rope = w_kv_a[..., kv_lora:]
    zpad = jnp.zeros(wk_rope.shape[:-1] + (LANES - rope_dim,), w_kv_a.dtype)
    wkv = jnp.concatenate([w_kv_a[..., :kv_lora], wk_rope, zpad, _swap_halves(wk_rope), zpad], axis=-1).astype(BF16)
    wuk = w_uk.reshape(w_uk.shape[0], kv_lora, n_heads * nope_dim).astype(BF16)
    wuv = w_uv.reshape(w_uv.shape[0], kv_lora, -1).astype(BF16)
    wo = w_o.astype(BF16)
    ln_g4, ln_b4 = ln_g[:, :, None, :], ln_b[:, :, None, :]
    g_q3, g_kv3 = g_q_a[:, None, :], g_kv_a[:, None, :]
    cache_krt = jnp.swapaxes(cache_k_rope, 2, 3)

    cos_m, sin_m = _rope_tables(jnp.tile(n_meta + jnp.arange(seq), nb), rope_dim)
    cos_e, sin_e = _rope_tables(jnp.concatenate([jnp.full((ns,), past), jnp.arange(n_meta)]), rope_dim)

    ffn_w4 = (w_ffn_gate, w_ffn_up, w_ffn_down)
    piggyback = _can_cast_next(nb * seq, d, w_ffn_gate.shape[-1])
    ready = {}

    def ffn_both(xm, xe, layer, half, ln_idx):
        if (layer, half) in ready:
            w = ready.pop((layer, half))
            xe = _ffn(xe, *w, ln_g4, ln_b4, layer, ln_idx, alpha)
        else:
            xe, *w = _ffn_cast(xe, *ffn_w4, ln_g4, ln_b4, layer, half, ln_idx, alpha)
        nxt = (layer, 1) if half == 0 else (layer + 1, 0)
        if piggyback and nxt[0] < depth:
            xm, *w_next = _ffn(xm, *w, ln_g4, ln_b4, layer, ln_idx, alpha, cast_next=ffn_w4 + nxt)
            ready[nxt] = w_next
        else:
            xm = _ffn(xm, *w, ln_g4, ln_b4, layer, ln_idx, alpha)
        return xm, xe

    conv_p, conv_s, ckv_p, ckv_s, kr_p, kr_s = [], [], [], [], [], []
    for i in range(depth):
        j = i // N_MIXERS
        xm, xe = ffn_both(xm, xe, i, 0, 0)
        if i % N_MIXERS == 0:
            dc = w_conv_out.shape[1]
            st = state_conv[j]
            zeros = jnp.zeros((n_meta, dc), F32)
            p1 = jnp.concatenate([st[:, 1], zeros], axis=0)
            p2 = jnp.concatenate([st[:, 0], zeros], axis=0)
            rows = jnp.arange(n_ext)[:, None]
            f1 = (rows <= ns).astype(F32)
            f2 = (rows <= ns + 1).astype(F32)
            xe, u_e, wb, wc, wh, wout = _conv_mixer_rows(xe, p1, p2, f1, f2, w_conv_in, w_conv, w_conv_out,
                                                         ln_g4, ln_b4, j, i, alpha)
            prev = jnp.broadcast_to(jnp.pad(u_e[n_ext - 2:], ((SUBLANES - 2, 0), (0, 0)))[None], (nb, SUBLANES, dc))
            xm, u_last = _conv_mixer_seq(xm, prev, wb, wc, wh, w_conv, wout, ln_g4, ln_b4, j, i, alpha, seq)
            tiles_per_seq = u_last.shape[0] // nb
            conv_p.append(u_last[tiles_per_seq - 1::tiles_per_seq, SUBLANES - 2:])
            conv_s.append(jnp.stack([st[:, 1], u_e[:ns]], axis=1))
        else:
            proj = functools.partial(_mla_proj, wqa=wqa, gqa=g_q3, wqn=wqn, wqr=wqr, wqrs=wqrs, wkv=wkv, gkv=g_kv3,
                                     wuk=wuk, wuv=wuv, j=j, n_heads=n_heads)
            qc_e, kc_e, v_e, ckv_e, kr_e = proj(xe, cos_e, sin_e)
            qc_m, kc_m, v_m, ckv_m, kr_m = proj(xm, cos_m, sin_m)
            o_meta = _prefix_attention(qc_e[ns:], kc_e[ns:], v_e[ns:], n_heads, scale)
            o_main = _attention(qc_m, kc_m, v_m, _pad_rows(kc_e[ns:], LANES), _pad_rows(v_e[ns:], LANES), n_meta,
                                nb, seq, n_heads, scale)
            q_lat = _absorb_q(qc_e, wuk, j, ns, n_heads).transpose(1, 0, 2)
            q_rope_s = qc_e[:ns].reshape(ns, n_heads, 2 * LANES)[:, :, LANES:LANES + rope_dim]
            o_lat = _sample_attention(q_lat, q_rope_s, ckv_e[:ns, None], kr_e[:ns, None],
                                      cache_kv_latent, cache_krt, page_table, j, scale)
            o_s = _expand_o(o_lat.transpose(1, 0, 2), wuv, j, n_heads)
            o_e = jnp.concatenate([o_s, o_meta], axis=0)
            xe = _proj_ln(o_e, xe, wo, ln_g4, ln_b4, j, i, alpha)
            xm = _proj_ln(o_main, xm, wo, ln_g4, ln_b4, j, i, alpha)
            meta_rows = lambda a: jnp.broadcast_to(a[ns:][None], (nb, n_meta, a.shape[-1]))
            ckv_p.append(jnp.concatenate([meta_rows(ckv_e), ckv_m.reshape(nb, seq, kv_lora)], axis=1))
            kr_p.append(jnp.concatenate([meta_rows(kr_e), kr_m.reshape(nb, seq, rope_dim)], axis=1))
            ckv_s.append(ckv_e[:ns, None])
            kr_s.append(kr_e[:ns, None])
        xm, xe = ffn_both(xm, xe, i, 1, 2)

    y_prompt = xm.reshape(nb, seq, d)
    y_sample = xe[:ns].reshape(ns, dec_seq, d)
    return (y_prompt, y_sample, jnp.stack(conv_p), jnp.stack(conv_s), jnp.stack(ckv_p), jnp.stack(ckv_s),
            jnp.stack(kr_p), jnp.stack(kr_s))
```

```python
import functools

import jax
import jax.numpy as jnp
from jax import lax
from jax.experimental import pallas as pl
from jax.experimental.pallas import tpu as pltpu

LN_EPS = 1e-5
RMS_EPS = 1e-6
ROPE_THETA = 10000.0
N_MIXERS = 2
LOG2_E = 1.4426950408889634
FFN_SUB = 256
CONV_TN = 512

LANES = 128
SUBLANES = 8
VMEM_LIMIT_BYTES = 56 * 1024 * 1024

BF16 = jnp.bfloat16
F32 = jnp.float32


def _tile(n, pref):
    if n <= pref:
        return n
    for t in range(pref, 15, -1):
        if n % t == 0 and t % 16 == 0:
            return t
    return n


def _params(*sem):
    return pltpu.CompilerParams(dimension_semantics=sem, vmem_limit_bytes=VMEM_LIMIT_BYTES)


def _resident(shape, index_map):
    return pl.BlockSpec(shape, index_map, pipeline_mode=pl.Buffered(1))


def _layer_norm(y, g, b):
    mu = jnp.mean(y, axis=-1, keepdims=True)
    yc = y - mu
    var = jnp.mean(yc * yc, axis=-1, keepdims=True)
    return yc * lax.rsqrt(var + LN_EPS) * g + b


def _rms_norm(y, g):
    return y * lax.rsqrt(jnp.mean(y * y, axis=-1, keepdims=True) + RMS_EPS) * g


def _dot(a, b):
    return jnp.dot(a, b, preferred_element_type=F32)


def _dot_nt(a, b):
    return lax.dot_general(a, b, (((1,), (1,)), ((), ())), preferred_element_type=F32)


def _ffn_chunk(xb, wg_ref, wu_ref, wd_ref):
    tf = wg_ref.shape[1]
    sub = FFN_SUB if tf % FFN_SUB == 0 else tf
    part = None
    for c in range(tf // sub):
        cols = slice(c * sub, (c + 1) * sub)
        gate = _dot(xb, wg_ref[:, cols])
        up = _dot(xb, wu_ref[:, cols])
        h = gate / (1.0 + jnp.exp(-gate)) * up
        d = _dot(h.astype(BF16), wd_ref[cols, :])
        part = d if part is None else part + d
    return part


def _ffn_step(f, n_f, x_ref, wg_ref, wu_ref, wd_ref, g_ref, b_ref, o_ref, xb_ref, acc_ref, alpha):
    @pl.when(f == 0)
    def _():
        xb_ref[...] = x_ref[...].astype(BF16)
        acc_ref[...] = jnp.zeros_like(acc_ref)

    acc_ref[...] += _ffn_chunk(xb_ref[...], wg_ref, wu_ref, wd_ref)

    @pl.when(f == n_f - 1)
    def _():
        y = alpha * x_ref[...] + 0.5 * acc_ref[...]
        o_ref[...] = _layer_norm(y, g_ref[...], b_ref[...])


def _ffn_deferred_ln_body(x_ref, wg_ref, wu_ref, wd_ref, g_ref, b_ref, o_ref, xb_ref, acc_ref, y_ref,
                          *, alpha, n_i, n_f):
    s = pl.program_id(0)
    i, f = s // n_f, s % n_f

    @pl.when(s == 0)
    def _():
        y_ref[...] = jnp.zeros_like(y_ref)

    @pl.when(jnp.logical_and(f == 0, i < n_i))
    def _():
        o_ref[...] = _layer_norm(y_ref[...], g_ref[...], b_ref[...])
        xb = x_ref[...].astype(BF16)
        xb_ref[...] = xb
        part = _ffn_chunk(xb, wg_ref, wu_ref, wd_ref)
        if n_f == 1:
            y_ref[...] = alpha * x_ref[...] + 0.5 * part
        else:
            acc_ref[...] = part

    @pl.when(i == n_i)
    def _():
        o_ref[...] = _layer_norm(y_ref[...], g_ref[...], b_ref[...])

    @pl.when(jnp.logical_and(f != 0, f < n_f - 1))
    def _():
        acc_ref[...] += _ffn_chunk(xb_ref[...], wg_ref, wu_ref, wd_ref)

    @pl.when(jnp.logical_and(f != 0, f == n_f - 1))
    def _():
        acc = acc_ref[...] + _ffn_chunk(xb_ref[...], wg_ref, wu_ref, wd_ref)
        y_ref[...] = alpha * x_ref[...] + 0.5 * acc


def _ffn_kernel(x_ref, wg_ref, wu_ref, wd_ref, g_ref, b_ref, o_ref, xb_ref, acc_ref, y_ref, **kw):
    _ffn_deferred_ln_body(x_ref, wg_ref, wu_ref, wd_ref, g_ref, b_ref, o_ref, xb_ref, acc_ref, y_ref, **kw)


def _ffn_cast_kernel(x_ref, wg_ref, wu_ref, wd_ref, g_ref, b_ref, o_ref, wgb_ref, wub_ref, wdb_ref,
                     xb_ref, acc_ref, *, alpha):
    wgb_ref[...] = wg_ref[...].astype(BF16)
    wub_ref[...] = wu_ref[...].astype(BF16)
    wdb_ref[...] = wd_ref[...].astype(BF16)
    _ffn_step(pl.program_id(0), pl.num_programs(0), x_ref, wgb_ref, wub_ref, wdb_ref,
              g_ref, b_ref, o_ref, xb_ref, acc_ref, alpha)


def _ffn_cast_next_kernel(x_ref, wg_ref, wu_ref, wd_ref, g_ref, b_ref, ng_ref, nu_ref, nd_ref,
                          o_ref, ngb_ref, nub_ref, ndb_ref, xb_ref, acc_ref, y_ref, **kw):
    ngb_ref[...] = ng_ref[...].astype(BF16)
    nub_ref[...] = nu_ref[...].astype(BF16)
    ndb_ref[...] = nd_ref[...].astype(BF16)
    _ffn_deferred_ln_body(x_ref, wg_ref, wu_ref, wd_ref, g_ref, b_ref, o_ref, xb_ref, acc_ref, y_ref, **kw)


def _ffn_tiles(m, d, ff, tm_pref=512, tf_pref=512):
    return _tile(m, tm_pref), _tile(ff, tf_pref)


def _can_cast_next(m, d, ff):
    tm, _ = _ffn_tiles(m, d, ff)
    n_i = m // tm
    return d % n_i == 0 and (d // n_i) % LANES == 0


def _ffn(x, wg, wu, wd, g, b, layer, ln_idx, alpha, cast_next=None):
    m, d = x.shape
    ff = wd.shape[0]
    tm, tf = _ffn_tiles(m, d, ff)
    n_i, n_f = m // tm, ff // tf
    assert wg.shape == (n_f, d, tf)
    last = n_i * n_f
    row = lambda s: jnp.minimum(s // n_f, n_i - 1)
    chunk = lambda s: jnp.where(s == last, n_f - 1, s % n_f)
    prev_row = lambda s: jnp.maximum(s // n_f - 1, 0)
    in_specs = [
        pl.BlockSpec((tm, d), lambda s: (row(s), 0)),
        pl.BlockSpec((None, d, tf), lambda s: (chunk(s), 0, 0)),
        pl.BlockSpec((None, d, tf), lambda s: (chunk(s), 0, 0)),
        pl.BlockSpec((tf, d), lambda s: (chunk(s), 0)),
        pl.BlockSpec((None, None, 1, d), lambda s: (layer, ln_idx, 0, 0)),
        pl.BlockSpec((None, None, 1, d), lambda s: (layer, ln_idx, 0, 0)),
    ]
    out_spec = pl.BlockSpec((tm, d), lambda s: (prev_row(s), 0))
    out_shape = jax.ShapeDtypeStruct((m, d), F32)
    scratch = [pltpu.VMEM((tm, d), BF16), pltpu.VMEM((tm, d), F32), pltpu.VMEM((tm, d), F32)]
    kw = dict(alpha=alpha, n_i=n_i, n_f=n_f)
    if cast_next is None:
        return pl.pallas_call(
            functools.partial(_ffn_kernel, **kw),
            grid=(last + 1,),
            in_specs=in_specs,
            out_specs=out_spec,
            out_shape=out_shape,
            scratch_shapes=scratch,
            compiler_params=_params("arbitrary"),
            name="ffn_macaron",
        )(x, wg, wu, wd, g, b)
    ng, nu, nd, layer_n, half_n = cast_next
    td = d // n_i
    return pl.pallas_call(
        functools.partial(_ffn_cast_next_kernel, **kw),
        grid=(last + 1,),
        in_specs=in_specs + [
            pl.BlockSpec((None, None, td, tf), lambda s: (layer_n, half_n, row(s), chunk(s))),
            pl.BlockSpec((None, None, td, tf), lambda s: (layer_n, half_n, row(s), chunk(s))),
            pl.BlockSpec((None, None, tf, td), lambda s: (layer_n, half_n, chunk(s), row(s))),
        ],
        out_specs=[out_spec,
                   pl.BlockSpec((None, td, tf), lambda s: (chunk(s), row(s), 0)),
                   pl.BlockSpec((None, td, tf), lambda s: (chunk(s), row(s), 0)),
                   pl.BlockSpec((tf, td), lambda s: (chunk(s), row(s)))],
        out_shape=[out_shape, jax.ShapeDtypeStruct((n_f, d, tf), BF16), jax.ShapeDtypeStruct((n_f, d, tf), BF16),
                   jax.ShapeDtypeStruct((ff, d), BF16)],
        scratch_shapes=scratch,
        compiler_params=_params("arbitrary"),
        name="ffn_macaron_cast_next",
    )(x, wg, wu, wd, g, b, ng, nu, nd)


def _ffn_cast(x, wg, wu, wd, g, b, layer, half, ln_idx, alpha, tf_pref=256):
    m, d = x.shape
    ff = wg.shape[-1]
    tf = _tile(ff, tf_pref)
    _, tf_main = _ffn_tiles(m, d, ff)
    assert tf_main % tf == 0
    per = tf_main // tf
    wide_out = pl.BlockSpec((None, d, tf), lambda f: (f // per, 0, f % per))
    wide_shape = jax.ShapeDtypeStruct((ff // tf_main, d, tf_main), BF16)
    return pl.pallas_call(
        functools.partial(_ffn_cast_kernel, alpha=alpha),
        grid=(ff // tf,),
        in_specs=[
            pl.BlockSpec((m, d), lambda f: (0, 0)),
            pl.BlockSpec((None, None, d, tf), lambda f: (layer, half, 0, f)),
            pl.BlockSpec((None, None, d, tf), lambda f: (layer, half, 0, f)),
            pl.BlockSpec((None, None, tf, d), lambda f: (layer, half, f, 0)),
            pl.BlockSpec((None, None, 1, d), lambda f: (layer, ln_idx, 0, 0)),
            pl.BlockSpec((None, None, 1, d), lambda f: (layer, ln_idx, 0, 0)),
        ],
        out_specs=[
            pl.BlockSpec((m, d), lambda f: (0, 0)),
            wide_out,
            wide_out,
            pl.BlockSpec((tf, d), lambda f: (f, 0)),
        ],
        out_shape=[jax.ShapeDtypeStruct((m, d), F32), wide_shape, wide_shape,
                   jax.ShapeDtypeStruct((ff, d), BF16)],
        scratch_shapes=[pltpu.VMEM((m, d), BF16), pltpu.VMEM((m, d), F32)],
        compiler_params=_params("arbitrary"),
        name="ffn_macaron_cast",
    )(x, wg, wu, wd, g, b)


def _conv_taps(u, u_m1, u_m2, wconv_ref):
    w = wconv_ref[...]
    return w[0:1] * u_m2 + w[1:2] * u_m1 + w[2:3] * u


def _conv_seq_kernel(x_ref, prev_ref, wb_ref, wc_ref, wh_ref, wconv_ref, wout_ref, g_ref, b_ref,
                     o_ref, ulast_ref, xb_ref, acc_ref, y_ref, carry_ref, *, alpha, tiles_per_seq, n_i, n_c):
    s = pl.program_id(0)
    i, c = s // n_c, s % n_c
    tm = x_ref.shape[0]

    def chunk(xb):
        u = _dot(xb, wc_ref[...]) * _dot(xb, wh_ref[...])
        seq_start = (i % tiles_per_seq) == 0
        tail = jnp.where(seq_start, prev_ref[...], carry_ref[c])
        row = lax.broadcasted_iota(jnp.int32, u.shape, 0)
        u_m1 = jnp.where(row == 0, tail[7:8], pltpu.roll(u, 1, 0))
        u_m2 = jnp.where(row == 0, tail[6:7], jnp.where(row == 1, tail[7:8], pltpu.roll(u, 2, 0)))
        conv = _conv_taps(u, u_m1, u_m2, wconv_ref)
        last = u[tm - SUBLANES:, :]
        carry_ref[c] = last
        ulast_ref[...] = last
        z = _dot(xb, wb_ref[...]) * conv
        return _dot(z.astype(BF16), wout_ref[...])

    @pl.when(s == 0)
    def _():
        y_ref[...] = jnp.zeros_like(y_ref)
        carry_ref[...] = jnp.zeros_like(carry_ref)

    @pl.when(jnp.logical_and(c == 0, i < n_i))
    def _():
        o_ref[...] = _layer_norm(y_ref[...], g_ref[...], b_ref[...])
        xb = x_ref[...].astype(BF16)
        xb_ref[...] = xb
        part = chunk(xb)
        if n_c == 1:
            y_ref[...] = alpha * x_ref[...] + part
        else:
            acc_ref[...] = part

    @pl.when(i == n_i)
    def _():
        o_ref[...] = _layer_norm(y_ref[...], g_ref[...], b_ref[...])

    @pl.when(jnp.logical_and(c != 0, c < n_c - 1))
    def _():
        acc_ref[...] += chunk(xb_ref[...])

    @pl.when(jnp.logical_and(c != 0, c == n_c - 1))
    def _():
        y_ref[...] = alpha * x_ref[...] + (acc_ref[...] + chunk(xb_ref[...]))


def _conv_rows_kernel(x_ref, p1_ref, p2_ref, f1_ref, f2_ref, wb_ref, wc_ref, wh_ref, wconv_ref, wout_ref,
                      g_ref, b_ref, o_ref, u_ref, wbb_ref, wcb_ref, whb_ref, woutb_ref, xb_ref, acc_ref, *, alpha):
    c = pl.program_id(0)

    @pl.when(c == 0)
    def _():
        xb_ref[...] = x_ref[...].astype(BF16)
        acc_ref[...] = jnp.zeros_like(acc_ref)

    wb, wc, wh = wb_ref[...].astype(BF16), wc_ref[...].astype(BF16), wh_ref[...].astype(BF16)
    wout = wout_ref[...].astype(BF16)
    wbb_ref[...] = wb
    wcb_ref[...] = wc
    whb_ref[...] = wh
    woutb_ref[...] = wout

    xb = xb_ref[...]
    u = _dot(xb, wc) * _dot(xb, wh)
    u_m1 = jnp.where(f1_ref[...] > 0.5, p1_ref[...], pltpu.roll(u, 1, 0))
    u_m2 = jnp.where(f2_ref[...] > 0.5, p2_ref[...], pltpu.roll(u, 2, 0))
    conv = _conv_taps(u, u_m1, u_m2, wconv_ref)
    u_ref[...] = u
    z = _dot(xb, wb) * conv
    acc_ref[...] += _dot(z.astype(BF16), wout)

    @pl.when(c == pl.num_programs(0) - 1)
    def _():
        o_ref[...] = _layer_norm(alpha * x_ref[...] + acc_ref[...], g_ref[...], b_ref[...])


def _conv_mixer_seq(x, prev, wb, wc, wh, w_conv, w_out, g, b, j, layer, alpha, seq_len, tm_pref=512):
    m, d = x.shape
    dc = w_out.shape[0]
    tm, tn = _tile(seq_len, tm_pref), _tile(dc, CONV_TN)
    tiles_per_seq = seq_len // tm
    n_i, n_c = m // tm, dc // tn
    assert wb.shape == (n_c, d, tn)
    last = n_i * n_c
    row = lambda s: jnp.minimum(s // n_c, n_i - 1)
    chunk = lambda s: jnp.where(s == last, n_c - 1, s % n_c)
    w_spec = pl.BlockSpec((None, d, tn), lambda s: (chunk(s), 0, 0))
    return pl.pallas_call(
        functools.partial(_conv_seq_kernel, alpha=alpha, tiles_per_seq=tiles_per_seq, n_i=n_i, n_c=n_c),
        grid=(last + 1,),
        in_specs=[
            pl.BlockSpec((tm, d), lambda s: (row(s), 0)),
            pl.BlockSpec((None, SUBLANES, tn), lambda s: (row(s) // tiles_per_seq, 0, chunk(s))),
            w_spec, w_spec, w_spec,
            pl.BlockSpec((None, 3, tn), lambda s: (j, 0, chunk(s))),
            pl.BlockSpec((tn, d), lambda s: (chunk(s), 0)),
            pl.BlockSpec((None, None, 1, d), lambda s: (layer, 1, 0, 0)),
            pl.BlockSpec((None, None, 1, d), lambda s: (layer, 1, 0, 0)),
        ],
        out_specs=[
            pl.BlockSpec((tm, d), lambda s: (jnp.maximum(s // n_c - 1, 0), 0)),
            pl.BlockSpec((None, SUBLANES, tn), lambda s: (row(s), 0, chunk(s))),
        ],
        out_shape=[jax.ShapeDtypeStruct((m, d), F32),
                   jax.ShapeDtypeStruct((n_i, SUBLANES, dc), F32)],
        scratch_shapes=[pltpu.VMEM((tm, d), BF16), pltpu.VMEM((tm, d), F32), pltpu.VMEM((tm, d), F32),
                        pltpu.VMEM((n_c, SUBLANES, tn), F32)],
        compiler_params=_params("arbitrary"),
        name="conv_mixer_seq",
    )(x, prev, wb, wc, wh, w_conv, w_out, g, b)


def _conv_mixer_rows(x, p1, p2, f1, f2, w_in, w_conv, w_out, g, b, j, layer, alpha, tn_pref=256):
    m, d = x.shape
    dc = w_out.shape[1]
    tn = _tile(dc, tn_pref)
    tn_main = _tile(dc, CONV_TN)
    assert tn_main % tn == 0
    per = tn_main // tn
    n_c = dc // tn
    w_spec = lambda part: pl.BlockSpec((None, d, tn), lambda c: (j, 0, part * n_c + c))
    wb_out = pl.BlockSpec((None, d, tn), lambda c: (c // per, 0, c % per))
    wide_shape = jax.ShapeDtypeStruct((dc // tn_main, d, tn_main), BF16)
    return pl.pallas_call(
        functools.partial(_conv_rows_kernel, alpha=alpha),
        grid=(n_c,),
        in_specs=[
            pl.BlockSpec((m, d), lambda c: (0, 0)),
            pl.BlockSpec((m, tn), lambda c: (0, c)),
            pl.BlockSpec((m, tn), lambda c: (0, c)),
            pl.BlockSpec((m, 1), lambda c: (0, 0)),
            pl.BlockSpec((m, 1), lambda c: (0, 0)),
            w_spec(0), w_spec(1), w_spec(2),
            pl.BlockSpec((None, 3, tn), lambda c: (j, 0, c)),
            pl.BlockSpec((None, tn, d), lambda c: (j, c, 0)),
            pl.BlockSpec((None, None, 1, d), lambda c: (layer, 1, 0, 0)),
            pl.BlockSpec((None, None, 1, d), lambda c: (layer, 1, 0, 0)),
        ],
        out_specs=[
            pl.BlockSpec((m, d), lambda c: (0, 0)),
            pl.BlockSpec((m, tn), lambda c: (0, c)),
            wb_out, wb_out, wb_out,
            pl.BlockSpec((tn, d), lambda c: (c, 0)),
        ],
        out_shape=[jax.ShapeDtypeStruct((m, d), F32), jax.ShapeDtypeStruct((m, dc), F32),
                   wide_shape, wide_shape, wide_shape, jax.ShapeDtypeStruct((dc, d), BF16)],
        scratch_shapes=[pltpu.VMEM((m, d), BF16), pltpu.VMEM((m, d), F32)],
        compiler_params=_params("arbitrary"),
        name="conv_mixer_rows",
    )(x, p1, p2, f1, f2, w_in, w_in, w_in, w_conv, w_out, g, b)


def _mla_proj_kernel(x_ref, cos_ref, sin_ref, wqa_ref, gqa_ref, wqn_ref, wqr_ref, wqrs_ref, wkv_ref, gkv_ref,
                     wuk_ref, wuv_ref, qc_ref, kc_ref, v_ref, ckv_ref, kr_ref, *, n_heads, kv_lora, rope_dim):
    half = LANES // 2
    assert rope_dim == half, "rope halves are packed two heads per 128 lanes"
    xb = x_ref[...].astype(BF16)
    cos, sin = cos_ref[...], sin_ref[...]
    lane = lax.broadcasted_iota(jnp.int32, cos.shape, 1)
    low = lane < half

    cq = _rms_norm(_dot(xb, wqa_ref[...]), gqa_ref[...]).astype(BF16)
    qn = _dot(cq, wqn_ref[...])
    qr_a = _dot(cq, wqr_ref[...])
    qr_b = _dot(cq, wqrs_ref[...])
    for pair in range(n_heads // 2):
        sl = slice(pair * LANES, (pair + 1) * LANES)
        both = qr_a[:, sl] * cos + qr_b[:, sl] * sin
        for k in range(2):
            h = 2 * pair + k
            part = both if k == 0 else pltpu.roll(both, half, 1)
            qc_ref[:, h * 2 * LANES:h * 2 * LANES + LANES] = qn[:, h * LANES:(h + 1) * LANES].astype(BF16)
            qc_ref[:, h * 2 * LANES + LANES:(h + 1) * 2 * LANES] = jnp.where(low, part, 0.0).astype(BF16)

    kv = _dot(xb, wkv_ref[...])
    ckv = _rms_norm(kv[:, :kv_lora], gkv_ref[...])
    ckv_ref[...] = ckv
    kr = kv[:, kv_lora:kv_lora + LANES] * cos + kv[:, kv_lora + LANES:] * sin
    kr = jnp.where(low, kr, 0.0)
    kr_ref[...] = kr[:, :rope_dim]
    ckv_b = ckv.astype(BF16)
    kn = _dot(ckv_b, wuk_ref[...])
    v = _dot(ckv_b, wuv_ref[...])
    kr_b = kr.astype(BF16)
    ones = jnp.ones((xb.shape[0], LANES), BF16)
    for h in range(n_heads):
        kc_ref[:, h * 2 * LANES:h * 2 * LANES + LANES] = kn[:, h * LANES:(h + 1) * LANES].astype(BF16)
        kc_ref[:, h * 2 * LANES + LANES:(h + 1) * 2 * LANES] = kr_b
        v_ref[:, h * 2 * LANES:h * 2 * LANES + LANES] = v[:, h * LANES:(h + 1) * LANES].astype(BF16)
        v_ref[:, h * 2 * LANES + LANES:(h + 1) * 2 * LANES] = ones


def _mla_proj(x, cos, sin, wqa, gqa, wqn, wqr, wqrs, wkv, gkv, wuk, wuv, j, n_heads, tm_pref=256):
    m, d = x.shape
    kv_lora = wuk.shape[1]
    rope_dim = wqr.shape[-1] // n_heads
    tm = _tile(m, tm_pref)
    hq = n_heads * 2 * LANES
    hv = hq
    row = lambda w: pl.BlockSpec((tm, w), lambda i: (i, 0))
    full = lambda a: _resident((None,) + a.shape[1:], lambda i: (j,) + (0,) * (a.ndim - 1))
    return pl.pallas_call(
        functools.partial(_mla_proj_kernel, n_heads=n_heads, kv_lora=kv_lora, rope_dim=rope_dim),
        grid=(m // tm,),
        in_specs=[row(d), row(LANES), row(LANES), full(wqa), full(gqa), full(wqn), full(wqr), full(wqrs),
                  full(wkv), full(gkv), full(wuk), full(wuv)],
        out_specs=[row(hq), row(hq), row(hv), row(kv_lora), row(rope_dim)],
        out_shape=[jax.ShapeDtypeStruct((m, hq), BF16), jax.ShapeDtypeStruct((m, hq), BF16),
                   jax.ShapeDtypeStruct((m, hv), BF16), jax.ShapeDtypeStruct((m, kv_lora), F32),
                   jax.ShapeDtypeStruct((m, rope_dim), F32)],
        compiler_params=_params("parallel"),
        name="mla_proj",
    )(x, cos, sin, wqa, gqa, wqn, wqr, wqrs, wkv, gkv, wuk, wuv)


def _lane_chunks(s):
    return [s[:, c * LANES:(c + 1) * LANES] for c in range(s.shape[1] // LANES)]


def _row_max(parts):
    m = functools.reduce(jnp.maximum, parts)
    return jnp.broadcast_to(jnp.max(m, axis=1, keepdims=True), m.shape)


def _attn_kernel(q_ref, k_ref, v_ref, kp_ref, vp_ref, o_ref, m_ref, acc_ref, *, c_exp, tk, n_pre):
    qi = pl.program_id(2)
    dq = 2 * LANES
    n_grp = q_ref.shape[1] // dq
    assert q_ref.shape[0] == tk and v_ref.shape[1] == n_grp * dq
    qs = [q_ref[:, h * dq:(h + 1) * dq] for h in range(n_grp)]
    hq = lambda h: slice(h * dq, (h + 1) * dq)
    k_blk = lambda h, start: k_ref[pl.ds(start, tk), hq(h)]
    v_blk = lambda h, start: v_ref[pl.ds(start, tk), hq(h)]
    hl = lambda h: slice(h * LANES, (h + 1) * LANES)

    start = pl.multiple_of(qi * tk, tk)
    row = lax.broadcasted_iota(jnp.int32, (tk, tk), 0)
    col = lax.broadcasted_iota(jnp.int32, (tk, tk), 1)
    causal = col <= row
    real_pre = lax.broadcasted_iota(jnp.int32, (tk, LANES), 1) < n_pre
    for h in range(n_grp):
        s_d = jnp.where(causal, _dot_nt(qs[h], k_blk(h, start)), -jnp.inf)
        s_p = jnp.where(real_pre, _dot_nt(qs[h], kp_ref[:, hq(h)]), -jnp.inf)
        parts = _lane_chunks(s_d) + [s_p]
        m = _row_max(parts)
        p_parts = [jnp.exp2((x - m) * c_exp) for x in parts]
        m_ref[:, hl(h)] = m
        acc_ref[:, hq(h)] = (_dot(jnp.concatenate(p_parts[:-1], axis=1).astype(BF16), v_blk(h, start))
                             + _dot(p_parts[-1].astype(BF16), vp_ref[:, hq(h)]))

    def block(jb):
        start = pl.multiple_of(jb * tk, tk)
        for h in range(n_grp):
            parts = _lane_chunks(_dot_nt(qs[h], k_blk(h, start)))
            m_old = m_ref[:, hl(h)]
            m_new = jnp.maximum(m_old, _row_max(parts))
            a = jnp.exp2((m_old - m_new) * c_exp)
            p = jnp.concatenate([jnp.exp2((x - m_new) * c_exp) for x in parts], axis=1).astype(BF16)
            acc_ref[:, hq(h)] = jnp.concatenate([a, a], axis=1) * acc_ref[:, hq(h)] + _dot(p, v_blk(h, start))
            m_ref[:, hl(h)] = m_new

    def pair(jp, carry):
        block(2 * jp)
        block(2 * jp + 1)
        return carry

    lax.fori_loop(0, qi // 2, pair, 0)

    @pl.when(qi % 2 == 1)
    def _():
        block(qi - 1)

    for h in range(n_grp):
        acc = acc_ref[:, hq(h)]
        o_ref[:, hl(h)] = (acc[:, :LANES] / acc[:, LANES:]).astype(o_ref.dtype)


def _attention(q, k, v, k_pre, v_pre, n_pre, n_seq, seq_len, n_heads, scale, t_pref=512, heads_per_step=4):
    dq = 2 * LANES
    t = _tile(seq_len, t_pref)
    n_t = seq_len // t
    g = max(x for x in range(1, heads_per_step + 1) if n_heads % x == 0)
    assert t % LANES == 0 and k_pre.shape[0] == LANES and v.shape[1] == n_heads * dq
    return pl.pallas_call(
        functools.partial(_attn_kernel, c_exp=scale * LOG2_E, tk=t, n_pre=n_pre),
        grid=(n_seq, n_heads // g, n_t),
        in_specs=[
            pl.BlockSpec((t, g * dq), lambda b, h, i: (b * n_t + i, h)),
            pl.BlockSpec((seq_len, g * dq), lambda b, h, i: (b, h)),
            pl.BlockSpec((seq_len, g * dq), lambda b, h, i: (b, h)),
            pl.BlockSpec((LANES, g * dq), lambda b, h, i: (0, h)),
            pl.BlockSpec((LANES, g * dq), lambda b, h, i: (0, h)),
        ],
        out_specs=pl.BlockSpec((t, g * LANES), lambda b, h, i: (b * n_t + i, h)),
        out_shape=jax.ShapeDtypeStruct((n_seq * seq_len, n_heads * LANES), BF16),
        scratch_shapes=[pltpu.VMEM((t, g * LANES), F32), pltpu.VMEM((t, g * dq), F32)],
        compiler_params=_params("parallel", "parallel", "arbitrary"),
        name="attn_causal",
    )(q, k, v, k_pre, v_pre)


def _prefix_attn_kernel(q_ref, k_ref, v_ref, o_ref, *, scale):
    s = _dot_nt(q_ref[...], k_ref[...]) * scale
    row = lax.broadcasted_iota(jnp.int32, s.shape, 0)
    col = lax.broadcasted_iota(jnp.int32, s.shape, 1)
    s = jnp.where(col <= row, s, -jnp.inf)
    p = jnp.exp(s - jnp.max(s, axis=1, keepdims=True))
    o = _dot(p.astype(BF16), v_ref[:, :LANES]) / jnp.sum(p, axis=1, keepdims=True)
    o_ref[...] = o.astype(o_ref.dtype)


def _prefix_attention(q, k, v, n_heads, scale):
    n = q.shape[0]
    dq = 2 * LANES
    blk = pl.BlockSpec((n, dq), lambda h: (0, h))
    return pl.pallas_call(
        functools.partial(_prefix_attn_kernel, scale=scale),
        grid=(n_heads,),
        in_specs=[blk, blk, blk],
        out_specs=pl.BlockSpec((n, LANES), lambda h: (0, h)),
        out_shape=jax.ShapeDtypeStruct((n, n_heads * LANES), BF16),
        compiler_params=_params("parallel"),
        name="attn_prefix",
    )(q, k, v)


def _absorb_q_kernel(q_ref, wuk_ref, o_ref):
    o_ref[...] = _dot_nt(q_ref[:, :LANES], wuk_ref[...]).astype(o_ref.dtype)


def _absorb_q(q_cat, wuk, j, n_rows, n_heads):
    kv_lora = wuk.shape[1]
    return pl.pallas_call(
        _absorb_q_kernel,
        grid=(n_heads,),
        in_specs=[pl.BlockSpec((n_rows, 2 * LANES), lambda h: (0, h)),
                  pl.BlockSpec((None, kv_lora, LANES), lambda h: (j, 0, h))],
        out_specs=pl.BlockSpec((None, n_rows, kv_lora), lambda h: (h, 0, 0)),
        out_shape=jax.ShapeDtypeStruct((n_heads, n_rows, kv_lora), BF16),
        compiler_params=_params("parallel"),
        name="absorb_q",
    )(q_cat, wuk)


def _expand_o_kernel(o_ref, wuv_ref, out_ref):
    out_ref[...] = _dot(o_ref[...], wuv_ref[...]).astype(out_ref.dtype)


def _expand_o(o_lat, wuv, j, n_heads):
    _, n_rows, kv_lora = o_lat.shape
    hv = wuv.shape[-1] // n_heads
    return pl.pallas_call(
        _expand_o_kernel,
        grid=(n_heads,),
        in_specs=[pl.BlockSpec((None, n_rows, kv_lora), lambda h: (h, 0, 0)),
                  pl.BlockSpec((None, kv_lora, hv), lambda h: (j, 0, h))],
        out_specs=pl.BlockSpec((n_rows, hv), lambda h: (0, h)),
        out_shape=jax.ShapeDtypeStruct((n_rows, n_heads * hv), BF16),
        compiler_params=_params("parallel"),
        name="expand_o",
    )(o_lat, wuv)


def _sample_attn_kernel(pt_ref, ql_ref, qr_ref, cnew_ref, krnew_ref, ckv_hbm, krt_hbm, o_ref,
                        ckv_buf, krt_buf, sem, *, scale, layer, n_pages):
    b = pl.program_id(0)
    slot = b % 2

    def page_copies(seq, slot, p):
        page_id = pt_ref[seq * n_pages + p]
        return (pltpu.make_async_copy(ckv_hbm.at[layer, page_id], ckv_buf.at[slot, p], sem.at[slot, 0]),
                pltpu.make_async_copy(krt_hbm.at[layer, page_id], krt_buf.at[slot, p], sem.at[slot, 1]))

    def start_gather(seq, slot):
        for p in range(n_pages):
            for cp in page_copies(seq, slot, p):
                cp.start()

    @pl.when(b == 0)
    def _():
        start_gather(0, 0)

    @pl.when(b + 1 < pl.num_programs(0))
    def _():
        start_gather(b + 1, 1 - slot)

    for p in range(n_pages):
        for cp in page_copies(b, slot, p):
            cp.wait()

    ql, qr = ql_ref[...], qr_ref[...]
    cn = cnew_ref[...].astype(BF16).astype(F32)
    kn = krnew_ref[...].astype(BF16).astype(F32)
    s_new = (jnp.sum(ql.astype(F32) * cn, axis=-1, keepdims=True)
             + jnp.sum(qr.astype(F32) * kn, axis=-1, keepdims=True)) * scale

    pages = [ckv_buf[slot, p].astype(BF16) for p in range(n_pages)]
    s = jnp.concatenate(
        [_dot_nt(ql, pages[p]) + _dot(qr, krt_buf[slot, p].astype(BF16)) for p in range(n_pages)],
        axis=1) * scale
    m = jnp.maximum(jnp.max(s, axis=-1, keepdims=True), s_new)
    e = jnp.exp(s - m)
    e_new = jnp.exp(s_new - m)
    eb = e.astype(BF16)
    page = pages[0].shape[0]
    acc = e_new.astype(BF16).astype(F32) * cn
    for p in range(n_pages):
        acc += _dot(eb[:, p * page:(p + 1) * page], pages[p])
    o_ref[...] = (acc / (jnp.sum(e, axis=-1, keepdims=True) + e_new)).astype(o_ref.dtype)


def _sample_attention(q_lat, q_rope, c_new, kr_new, cache_ckv, cache_krt, page_table, j, scale):
    n, n_heads, kv_lora = q_lat.shape
    rope_dim = q_rope.shape[-1]
    n_pages, page = page_table.shape[1], cache_ckv.shape[2]
    per_row = lambda s1, w: pl.BlockSpec((None, s1, w), lambda b, pt: (b, 0, 0))
    hbm = pl.BlockSpec(memory_space=pl.ANY)
    grid_spec = pltpu.PrefetchScalarGridSpec(
        num_scalar_prefetch=1,
        grid=(n,),
        in_specs=[per_row(n_heads, kv_lora), per_row(n_heads, rope_dim), per_row(1, kv_lora), per_row(1, rope_dim),
                  hbm, hbm],
        out_specs=per_row(n_heads, kv_lora),
        scratch_shapes=[pltpu.VMEM((2, n_pages, page, kv_lora), cache_ckv.dtype),
                        pltpu.VMEM((2, n_pages, rope_dim, page), cache_krt.dtype),
                        pltpu.SemaphoreType.DMA((2, 2))],
    )
    return pl.pallas_call(
        functools.partial(_sample_attn_kernel, scale=scale, layer=j, n_pages=n_pages),
        grid_spec=grid_spec,
        out_shape=jax.ShapeDtypeStruct((n, n_heads, kv_lora), BF16),
        compiler_params=_params("arbitrary"),
        name="sample_attn",
    )(page_table.reshape(-1), q_lat, q_rope, c_new, kr_new, cache_ckv, cache_krt)


def _proj_ln_kernel(a_ref, x_ref, w_ref, g_ref, b_ref, o_ref, y_ref, *, alpha, n_i):
    s = pl.program_id(0)

    @pl.when(s == 0)
    def _():
        y_ref[...] = jnp.zeros_like(y_ref)

    @pl.when(s < n_i)
    def _():
        o_ref[...] = _layer_norm(y_ref[...], g_ref[...], b_ref[...])
        y_ref[...] = alpha * x_ref[...] + _dot(a_ref[...], w_ref[...])

    @pl.when(s == n_i)
    def _():
        o_ref[...] = _layer_norm(y_ref[...], g_ref[...], b_ref[...])


def _proj_ln(a, x, w, g, b, j, layer, alpha, tm_pref=512):
    m, d = x.shape
    k = a.shape[1]
    tm = _tile(m, tm_pref)
    n_i = m // tm
    row = lambda s: jnp.minimum(s, n_i - 1)
    return pl.pallas_call(
        functools.partial(_proj_ln_kernel, alpha=alpha, n_i=n_i),
        grid=(n_i + 1,),
        in_specs=[
            pl.BlockSpec((tm, k), lambda s: (row(s), 0)),
            pl.BlockSpec((tm, d), lambda s: (row(s), 0)),
            _resident((None, k, d), lambda s: (j, 0, 0)),
            pl.BlockSpec((None, None, 1, d), lambda s: (layer, 1, 0, 0)),
            pl.BlockSpec((None, None, 1, d), lambda s: (layer, 1, 0, 0)),
        ],
        out_specs=pl.BlockSpec((tm, d), lambda s: (jnp.maximum(s - 1, 0), 0)),
        out_shape=jax.ShapeDtypeStruct((m, d), F32),
        scratch_shapes=[pltpu.VMEM((tm, d), F32)],
        compiler_params=_params("arbitrary"),
        name="proj_ln",
    )(a, x, w, g, b)


def _rope_tables(pos, rope_dim):
    inv = ROPE_THETA ** (-jnp.arange(0, rope_dim, 2, dtype=F32) / rope_dim)
    ang = pos.astype(F32)[:, None] * inv[None, :]
    cos, sin = jnp.cos(ang), jnp.sin(ang)
    reps = LANES // rope_dim
    return jnp.tile(jnp.concatenate([cos, cos], -1), (1, reps)), jnp.tile(jnp.concatenate([-sin, sin], -1), (1, reps))


def _swap_halves(w):
    half = w.shape[-1] // 2
    return jnp.concatenate([w[..., half:], w[..., :half]], axis=-1)


def _pad_rows(a, rows):
    return jnp.pad(a, ((0, rows - a.shape[0]), (0, 0)))


def kernel(x_prompt, x_sample, state_conv, cache_kv_latent, cache_k_rope, page_table, meta_tokens, ln_g, ln_b, w_ffn_gate, w_ffn_up, w_ffn_down, w_conv_in, w_conv, w_conv_out, w_q_a, g_q_a, w_q_b, w_kv_a, g_kv_a, w_uk, w_uv, w_o):
    nb, seq, d = x_prompt.shape
    ns, dec_seq, _ = x_sample.shape
    assert dec_seq == 1, "the sample path handles one new token per sequence"
    n_meta = meta_tokens.shape[0]
    depth = ln_g.shape[0]
    alpha = float((2 * depth) ** 0.25)
    n_heads = w_q_b.shape[2]
    rope_dim = cache_k_rope.shape[-1]
    nope_dim = w_q_b.shape[3] - rope_dim
    kv_lora = w_uk.shape[1]
    scale = float((nope_dim + rope_dim) ** -0.5)
    past = page_table.shape[1] * cache_kv_latent.shape[2]
    assert nope_dim == LANES and w_uv.shape[-1] == LANES and n_heads % 2 == 0 and n_meta <= LANES

    xm = x_prompt.reshape(nb * seq, d)
    xe = jnp.concatenate([x_sample.reshape(ns, d), meta_tokens.astype(x_prompt.dtype)], axis=0)
    n_ext = ns + n_meta

    wqa = w_q_a.astype(BF16)
    wqn = w_q_b[..., :nope_dim].reshape(w_q_b.shape[0], w_q_b.shape[1], n_heads * nope_dim).astype(BF16)
    wq_rope = w_q_b[..., nope_dim:]
    wqr = wq_rope.reshape(w_q_b.shape[0], w_q_b.shape[1], n_heads * rope_dim).astype(BF16)
    wqrs = _swap_halves(wq_rope).reshape(wqr.shape).astype(BF16)
    wk_rope = w_kv_a[..., kv_lora:]
    zpad = jnp.zeros(wk_rope.shape[:-1] + (LANES - rope_dim,), w_kv_a.dtype)
    wkv = jnp.concatenate([w_kv_a[..., :kv_lora], wk_rope, zpad, _swap_halves(wk_rope), zpad], axis=-1).astype(BF16)
    wuk = w_uk.reshape(w_uk.shape[0], kv_lora, n_heads * nope_dim).astype(BF16)
    wuv = w_uv.reshape(w_uv.shape[0], kv_lora, -1).astype(BF16)
    wo = w_o.astype(BF16)
    ln_g4, ln_b4 = ln_g[:, :, None, :], ln_b[:, :, None, :]
    g_q3, g_kv3 = g_q_a[:, None, :], g_kv_a[:, None, :]
    cache_krt = jnp.swapaxes(cache_k_rope, 2, 3)

    cos_m, sin_m = _rope_tables(jnp.tile(n_meta + jnp.arange(seq), nb), rope_dim)
    cos_e, sin_e = _rope_tables(jnp.concatenate([jnp.full((ns,), past), jnp.arange(n_meta)]), rope_dim)

    ffn_w4 = (w_ffn_gate, w_ffn_up, w_ffn_down)
    piggyback = _can_cast_next(nb * seq, d, w_ffn_gate.shape[-1])
    ready = {}

    def ffn_both(xm, xe, layer, half, ln_idx):
        if (layer, half) in ready:
            w = ready.pop((layer, half))
            xe = _ffn(xe, *w, ln_g4, ln_b4, layer, ln_idx, alpha)
        else:
            xe, *w = _ffn_cast(xe, *ffn_w4, ln_g4, ln_b4, layer, half, ln_idx, alpha)
        nxt = (layer, 1) if half == 0 else (layer + 1, 0)
        if piggyback and nxt[0] < depth:
            xm, *w_next = _ffn(xm, *w, ln_g4, ln_b4, layer, ln_idx, alpha, cast_next=ffn_w4 + nxt)
            ready[nxt] = w_next
        else:
            xm = _ffn(xm, *w, ln_g4, ln_b4, layer, ln_idx, alpha)
        return xm, xe

    conv_p, conv_s, ckv_p, ckv_s, kr_p, kr_s = [], [], [], [], [], []
    for i in range(depth):
        j = i // N_MIXERS
        xm, xe = ffn_both(xm, xe, i, 0, 0)
        if i % N_MIXERS == 0:
            dc = w_conv_out.shape[1]
            st = state_conv[j]
            zeros = jnp.zeros((n_meta, dc), F32)
            p1 = jnp.concatenate([st[:, 1], zeros], axis=0)
            p2 = jnp.concatenate([st[:, 0], zeros], axis=0)
            rows = jnp.arange(n_ext)[:, None]
            f1 = (rows <= ns).astype(F32)
            f2 = (rows <= ns + 1).astype(F32)
            xe, u_e, wb, wc, wh, wout = _conv_mixer_rows(xe, p1, p2, f1, f2, w_conv_in, w_conv, w_conv_out,
                                                         ln_g4, ln_b4, j, i, alpha)
            prev = jnp.broadcast_to(jnp.pad(u_e[n_ext - 2:], ((SUBLANES - 2, 0), (0, 0)))[None], (nb, SUBLANES, dc))
            xm, u_last = _conv_mixer_seq(xm, prev, wb, wc, wh, w_conv, wout, ln_g4, ln_b4, j, i, alpha, seq)
            tiles_per_seq = u_last.shape[0] // nb
            conv_p.append(u_last[tiles_per_seq - 1::tiles_per_seq, SUBLANES - 2:])
            conv_s.append(jnp.stack([st[:, 1], u_e[:ns]], axis=1))
        else:
            proj = functools.partial(_mla_proj, wqa=wqa, gqa=g_q3, wqn=wqn, wqr=wqr, wqrs=wqrs, wkv=wkv, gkv=g_kv3,
                                     wuk=wuk, wuv=wuv, j=j, n_heads=n_heads)
            qc_e, kc_e, v_e, ckv_e, kr_e = proj(xe, cos_e, sin_e)
            qc_m, kc_m, v_m, ckv_m, kr_m = proj(xm, cos_m, sin_m)
            o_meta = _prefix_attention(qc_e[ns:], kc_e[ns:], v_e[ns:], n_heads, scale)
            o_main = _attention(qc_m, kc_m, v_m, _pad_rows(kc_e[ns:], LANES), _pad_rows(v_e[ns:], LANES), n_meta,
                                nb, seq, n_heads, scale)
            q_lat = _absorb_q(qc_e, wuk, j, ns, n_heads).transpose(1, 0, 2)
            q_rope_s = qc_e[:ns].reshape(ns, n_heads, 2 * LANES)[:, :, LANES:LANES + rope_dim]
            o_lat = _sample_attention(q_lat, q_rope_s, ckv_e[:ns, None], kr_e[:ns, None],
                                      cache_kv_latent, cache_krt, page_table, j, scale)
            o_s = _expand_o(o_lat.transpose(1, 0, 2), wuv, j, n_heads)
            o_e = jnp.concatenate([o_s, o_meta], axis=0)
            xe = _proj_ln(o_e, xe, wo, ln_g4, ln_b4, j, i, alpha)
            xm = _proj_ln(o_main, xm, wo, ln_g4, ln_b4, j, i, alpha)
            meta_rows = lambda a: jnp.broadcast_to(a[ns:][None], (nb, n_meta, a.shape[-1]))
            ckv_p.append(jnp.concatenate([meta_rows(ckv_e), ckv_m.reshape(nb, seq, kv_lora)], axis=1))
            kr_p.append(jnp.concatenate([meta_rows(kr_e), kr_m.reshape(nb, seq, rope_dim)], axis=1))
            ckv_s.append(ckv_e[:ns, None])
            kr_s.append(kr_e[:ns, None])
        xm, xe = ffn_both(xm, xe, i, 1, 2)

    y_prompt = xm.reshape(nb, seq, d)
    y_sample = xe[:ns].reshape(ns, dec_seq, d)
    return (y_prompt, y_sample, jnp.stack(conv_p), jnp.stack(conv_s), jnp.stack(ckv_p), jnp.stack(ckv_s),
            jnp.stack(kr_p), jnp.stack(kr_s))
```

```python
import functools

import jax
import jax.numpy as jnp
from jax import lax
from jax.experimental import pallas as pl
from jax.experimental.pallas import tpu as pltpu

LN_EPS = 1e-5
RMS_EPS = 1e-6
ROPE_THETA = 10000.0
N_MIXERS = 2
LOG2_E = 1.4426950408889634
FFN_SUB = 256
CONV_TN = 512

LANES = 128
SUBLANES = 8
VMEM_LIMIT_BYTES = 56 * 1024 * 1024

BF16 = jnp.bfloat16
F32 = jnp.float32


def _tile(n, pref):
    if n <= pref:
        return n
    for t in range(pref, 15, -1):
        if n % t == 0 and t % 16 == 0:
            return t
    return n


def _params(*sem):
    return pltpu.CompilerParams(dimension_semantics=sem, vmem_limit_bytes=VMEM_LIMIT_BYTES)


def _resident(shape, index_map):
    return pl.BlockSpec(shape, index_map, pipeline_mode=pl.Buffered(1))


def _layer_norm(y, g, b):
    mu = jnp.mean(y, axis=-1, keepdims=True)
    yc = y - mu
    var = jnp.mean(yc * yc, axis=-1, keepdims=True)
    return yc * lax.rsqrt(var + LN_EPS) * g + b


def _rms_norm(y, g):
    return y * lax.rsqrt(jnp.mean(y * y, axis=-1, keepdims=True) + RMS_EPS) * g


def _dot(a, b):
    return jnp.dot(a, b, preferred_element_type=F32)


def _dot_nt(a, b):
    return lax.dot_general(a, b, (((1,), (1,)), ((), ())), preferred_element_type=F32)


def _ffn_chunk(xb, wg_ref, wu_ref, wd_ref):
    tf = wg_ref.shape[1]
    sub = FFN_SUB if tf % FFN_SUB == 0 else tf
    part = None
    for c in range(tf // sub):
        cols = slice(c * sub, (c + 1) * sub)
        gate = _dot(xb, wg_ref[:, cols])
        up = _dot(xb, wu_ref[:, cols])
        h = gate / (1.0 + jnp.exp(-gate)) * up
        d = _dot(h.astype(BF16), wd_ref[cols, :])
        part = d if part is None else part + d
    return part


def _ffn_step(f, n_f, x_ref, wg_ref, wu_ref, wd_ref, g_ref, b_ref, o_ref, xb_ref, acc_ref, alpha):
    @pl.when(f == 0)
    def _():
        xb_ref[...] = x_ref[...].astype(BF16)
        acc_ref[...] = jnp.zeros_like(acc_ref)

    acc_ref[...] += _ffn_chunk(xb_ref[...], wg_ref, wu_ref, wd_ref)

    @pl.when(f == n_f - 1)
    def _():
        y = alpha * x_ref[...] + 0.5 * acc_ref[...]
        o_ref[...] = _layer_norm(y, g_ref[...], b_ref[...])


def _ffn_deferred_ln_body(x_ref, wg_ref, wu_ref, wd_ref, g_ref, b_ref, o_ref, xb_ref, acc_ref, y_ref,
                          *, alpha, n_i, n_f):
    s = pl.program_id(0)
    i, f = s // n_f, s % n_f

    @pl.when(s == 0)
    def _():
        y_ref[...] = jnp.zeros_like(y_ref)

    @pl.when(jnp.logical_and(f == 0, i < n_i))
    def _():
        o_ref[...] = _layer_norm(y_ref[...], g_ref[...], b_ref[...])
        xb = x_ref[...].astype(BF16)
        xb_ref[...] = xb
        part = _ffn_chunk(xb, wg_ref, wu_ref, wd_ref)
        if n_f == 1:
            y_ref[...] = alpha * x_ref[...] + 0.5 * part
        else:
            acc_ref[...] = part

    @pl.when(i == n_i)
    def _():
        o_ref[...] = _layer_norm(y_ref[...], g_ref[...], b_ref[...])

    @pl.when(jnp.logical_and(f != 0, f < n_f - 1))
    def _():
        acc_ref[...] += _ffn_chunk(xb_ref[...], wg_ref, wu_ref, wd_ref)

    @pl.when(jnp.logical_and(f != 0, f == n_f - 1))
    def _():
        acc = acc_ref[...] + _ffn_chunk(xb_ref[...], wg_ref, wu_ref, wd_ref)
        y_ref[...] = alpha * x_ref[...] + 0.5 * acc


def _ffn_kernel(*refs, alpha, n_i, n_f, has_ext, has_cast):
    it = iter(refs)
    x_ref, wg_ref, wu_ref, wd_ref, g_ref, b_ref = (next(it) for _ in range(6))
    xe_ref = next(it) if has_ext else None
    casts_in = [next(it) for _ in range(3)] if has_cast else []
    o_ref = next(it)
    oe_ref = next(it) if has_ext else None
    casts_out = [next(it) for _ in range(3)] if has_cast else []
    xb_ref, acc_ref, y_ref = (next(it) for _ in range(3))
    for src, dst in zip(casts_in, casts_out):
        dst[...] = src[...].astype(BF16)
    _ffn_deferred_ln_body(x_ref, wg_ref, wu_ref, wd_ref, g_ref, b_ref, o_ref, xb_ref, acc_ref, y_ref,
                          alpha=alpha, n_i=n_i, n_f=n_f)
    if has_ext:
        xeb_ref, acce_ref = next(it), next(it)
        s = pl.program_id(0)

        @pl.when(s < n_f)
        def _():
            _ffn_step(s, n_f, xe_ref, wg_ref, wu_ref, wd_ref, g_ref, b_ref, oe_ref, xeb_ref, acce_ref, alpha)


def _ffn_cast_kernel(x_ref, wg_ref, wu_ref, wd_ref, g_ref, b_ref, o_ref, wgb_ref, wub_ref, wdb_ref,
                     xb_ref, acc_ref, *, alpha):
    wgb_ref[...] = wg_ref[...].astype(BF16)
    wub_ref[...] = wu_ref[...].astype(BF16)
    wdb_ref[...] = wd_ref[...].astype(BF16)
    _ffn_step(pl.program_id(0), pl.num_programs(0), x_ref, wgb_ref, wub_ref, wdb_ref,
              g_ref, b_ref, o_ref, xb_ref, acc_ref, alpha)


def _ffn_tiles(m, d, ff, tm_pref=512, tf_pref=512):
    return _tile(m, tm_pref), _tile(ff, tf_pref)


def _can_cast_next(m, d, ff):
    tm, _ = _ffn_tiles(m, d, ff)
    n_i = m // tm
    return d % n_i == 0 and (d // n_i) % LANES == 0


def _ffn(x, wg, wu, wd, g, b, layer, ln_idx, alpha, ext=None, cast_next=None):
    m, d = x.shape
    ff = wd.shape[0]
    tm, tf = _ffn_tiles(m, d, ff)
    n_i, n_f = m // tm, ff // tf
    assert wg.shape == (n_f, d, tf)
    last = n_i * n_f
    row = lambda s: jnp.minimum(s // n_f, n_i - 1)
    chunk = lambda s: jnp.where(s == last, n_f - 1, s % n_f)
    prev_row = lambda s: jnp.maximum(s // n_f - 1, 0)
    in_specs = [
        pl.BlockSpec((tm, d), lambda s: (row(s), 0)),
        pl.BlockSpec((None, d, tf), lambda s: (chunk(s), 0, 0)),
        pl.BlockSpec((None, d, tf), lambda s: (chunk(s), 0, 0)),
        pl.BlockSpec((tf, d), lambda s: (chunk(s), 0)),
        pl.BlockSpec((None, None, 1, d), lambda s: (layer, ln_idx, 0, 0)),
        pl.BlockSpec((None, None, 1, d), lambda s: (layer, ln_idx, 0, 0)),
    ]
    args = [x, wg, wu, wd, g, b]
    out_specs = [pl.BlockSpec((tm, d), lambda s: (prev_row(s), 0))]
    out_shape = [jax.ShapeDtypeStruct((m, d), F32)]
    scratch = [pltpu.VMEM((tm, d), BF16), pltpu.VMEM((tm, d), F32), pltpu.VMEM((tm, d), F32)]
    if ext is not None:
        me = ext.shape[0]
        whole = pl.BlockSpec((me, d), lambda s: (0, 0))
        in_specs.append(whole)
        args.append(ext)
        out_specs.append(whole)
        out_shape.append(jax.ShapeDtypeStruct((me, d), F32))
    if cast_next is not None:
        ng, nu, nd, layer_n, half_n = cast_next
        td = d // n_i
        in_specs += [
            pl.BlockSpec((None, None, td, tf), lambda s: (layer_n, half_n, row(s), chunk(s))),
            pl.BlockSpec((None, None, td, tf), lambda s: (layer_n, half_n, row(s), chunk(s))),
            pl.BlockSpec((None, None, tf, td), lambda s: (layer_n, half_n, chunk(s), row(s))),
        ]
        args += [ng, nu, nd]
        out_specs += [pl.BlockSpec((None, td, tf), lambda s: (chunk(s), row(s), 0)),
                      pl.BlockSpec((None, td, tf), lambda s: (chunk(s), row(s), 0)),
                      pl.BlockSpec((tf, td), lambda s: (chunk(s), row(s)))]
        out_shape += [jax.ShapeDtypeStruct((n_f, d, tf), BF16), jax.ShapeDtypeStruct((n_f, d, tf), BF16),
                      jax.ShapeDtypeStruct((ff, d), BF16)]
    if ext is not None:
        scratch += [pltpu.VMEM((me, d), BF16), pltpu.VMEM((me, d), F32)]
    return pl.pallas_call(
        functools.partial(_ffn_kernel, alpha=alpha, n_i=n_i, n_f=n_f,
                          has_ext=ext is not None, has_cast=cast_next is not None),
        grid=(last + 1,),
        in_specs=in_specs,
        out_specs=out_specs,
        out_shape=out_shape,
        scratch_shapes=scratch,
        compiler_params=_params("arbitrary"),
        name="ffn_macaron",
    )(*args)


def _ffn_cast(x, wg, wu, wd, g, b, layer, half, ln_idx, alpha, tf_pref=256):
    m, d = x.shape
    ff = wg.shape[-1]
    tf = _tile(ff, tf_pref)
    _, tf_main = _ffn_tiles(m, d, ff)
    assert tf_main % tf == 0
    per = tf_main // tf
    wide_out = pl.BlockSpec((None, d, tf), lambda f: (f // per, 0, f % per))
    wide_shape = jax.ShapeDtypeStruct((ff // tf_main, d, tf_main), BF16)
    return pl.pallas_call(
        functools.partial(_ffn_cast_kernel, alpha=alpha),
        grid=(ff // tf,),
        in_specs=[
            pl.BlockSpec((m, d), lambda f: (0, 0)),
            pl.BlockSpec((None, None, d, tf), lambda f: (layer, half, 0, f)),
            pl.BlockSpec((None, None, d, tf), lambda f: (layer, half, 0, f)),
            pl.BlockSpec((None, None, tf, d), lambda f: (layer, half, f, 0)),
            pl.BlockSpec((None, None, 1, d), lambda f: (layer, ln_idx, 0, 0)),
            pl.BlockSpec((None, None, 1, d), lambda f: (layer, ln_idx, 0, 0)),
        ],
        out_specs=[
            pl.BlockSpec((m, d), lambda f: (0, 0)),
            wide_out,
            wide_out,
            pl.BlockSpec((tf, d), lambda f: (f, 0)),
        ],
        out_shape=[jax.ShapeDtypeStruct((m, d), F32), wide_shape, wide_shape,
                   jax.ShapeDtypeStruct((ff, d), BF16)],
        scratch_shapes=[pltpu.VMEM((m, d), BF16), pltpu.VMEM((m, d), F32)],
        compiler_params=_params("arbitrary"),
        name="ffn_macaron_cast",
    )(x, wg, wu, wd, g, b)


def _conv_taps(u, u_m1, u_m2, wconv_ref):
    w = wconv_ref[...]
    return w[0:1] * u_m2 + w[1:2] * u_m1 + w[2:3] * u


def _conv_seq_kernel(x_ref, prev_ref, wb_ref, wc_ref, wh_ref, wconv_ref, wout_ref, g_ref, b_ref,
                     o_ref, ulast_ref, xb_ref, acc_ref, y_ref, carry_ref, *, alpha, tiles_per_seq, n_i, n_c):
    s = pl.program_id(0)
    i, c = s // n_c, s % n_c
    tm = x_ref.shape[0]

    def chunk(xb):
        u = _dot(xb, wc_ref[...]) * _dot(xb, wh_ref[...])
        seq_start = (i % tiles_per_seq) == 0
        tail = jnp.where(seq_start, prev_ref[...], carry_ref[c])
        row = lax.broadcasted_iota(jnp.int32, u.shape, 0)
        u_m1 = jnp.where(row == 0, tail[7:8], pltpu.roll(u, 1, 0))
        u_m2 = jnp.where(row == 0, tail[6:7], jnp.where(row == 1, tail[7:8], pltpu.roll(u, 2, 0)))
        conv = _conv_taps(u, u_m1, u_m2, wconv_ref)
        last = u[tm - SUBLANES:, :]
        carry_ref[c] = last
        ulast_ref[...] = last
        z = _dot(xb, wb_ref[...]) * conv
        return _dot(z.astype(BF16), wout_ref[...])

    @pl.when(s == 0)
    def _():
        y_ref[...] = jnp.zeros_like(y_ref)
        carry_ref[...] = jnp.zeros_like(carry_ref)

    @pl.when(jnp.logical_and(c == 0, i < n_i))
    def _():
        o_ref[...] = _layer_norm(y_ref[...], g_ref[...], b_ref[...])
        xb = x_ref[...].astype(BF16)
        xb_ref[...] = xb
        part = chunk(xb)
        if n_c == 1:
            y_ref[...] = alpha * x_ref[...] + part
        else:
            acc_ref[...] = part

    @pl.when(i == n_i)
    def _():
        o_ref[...] = _layer_norm(y_ref[...], g_ref[...], b_ref[...])

    @pl.when(jnp.logical_and(c != 0, c < n_c - 1))
    def _():
        acc_ref[...] += chunk(xb_ref[...])

    @pl.when(jnp.logical_and(c != 0, c == n_c - 1))
    def _():
        y_ref[...] = alpha * x_ref[...] + (acc_ref[...] + chunk(xb_ref[...]))


def _conv_rows_kernel(x_ref, p1_ref, p2_ref, f1_ref, f2_ref, wb_ref, wc_ref, wh_ref, wconv_ref, wout_ref,
                      g_ref, b_ref, o_ref, u_ref, wbb_ref, wcb_ref, whb_ref, woutb_ref, xb_ref, acc_ref, *, alpha):
    c = pl.program_id(0)

    @pl.when(c == 0)
    def _():
        xb_ref[...] = x_ref[...].astype(BF16)
        acc_ref[...] = jnp.zeros_like(acc_ref)

    wb, wc, wh = wb_ref[...].astype(BF16), wc_ref[...].astype(BF16), wh_ref[...].astype(BF16)
    wout = wout_ref[...].astype(BF16)
    wbb_ref[...] = wb
    wcb_ref[...] = wc
    whb_ref[...] = wh
    woutb_ref[...] = wout

    xb = xb_ref[...]
    u = _dot(xb, wc) * _dot(xb, wh)
    u_m1 = jnp.where(f1_ref[...] > 0.5, p1_ref[...], pltpu.roll(u, 1, 0))
    u_m2 = jnp.where(f2_ref[...] > 0.5, p2_ref[...], pltpu.roll(u, 2, 0))
    conv = _conv_taps(u, u_m1, u_m2, wconv_ref)
    u_ref[...] = u
    z = _dot(xb, wb) * conv
    acc_ref[...] += _dot(z.astype(BF16), wout)

    @pl.when(c == pl.num_programs(0) - 1)
    def _():
        o_ref[...] = _layer_norm(alpha * x_ref[...] + acc_ref[...], g_ref[...], b_ref[...])


def _conv_mixer_seq(x, prev, wb, wc, wh, w_conv, w_out, g, b, j, layer, alpha, seq_len, tm_pref=512):
    m, d = x.shape
    dc = w_out.shape[0]
    tm, tn = _tile(seq_len, tm_pref), _tile(dc, CONV_TN)
    tiles_per_seq = seq_len // tm
    n_i, n_c = m // tm, dc // tn
    assert wb.shape == (n_c, d, tn)
    last = n_i * n_c
    row = lambda s: jnp.minimum(s // n_c, n_i - 1)
    chunk = lambda s: jnp.where(s == last, n_c - 1, s % n_c)
    w_spec = pl.BlockSpec((None, d, tn), lambda s: (chunk(s), 0, 0))
    return pl.pallas_call(
        functools.partial(_conv_seq_kernel, alpha=alpha, tiles_per_seq=tiles_per_seq, n_i=n_i, n_c=n_c),
        grid=(last + 1,),
        in_specs=[
            pl.BlockSpec((tm, d), lambda s: (row(s), 0)),
            pl.BlockSpec((None, SUBLANES, tn), lambda s: (row(s) // tiles_per_seq, 0, chunk(s))),
            w_spec, w_spec, w_spec,
            pl.BlockSpec((None, 3, tn), lambda s: (j, 0, chunk(s))),
            pl.BlockSpec((tn, d), lambda s: (chunk(s), 0)),
            pl.BlockSpec((None, None, 1, d), lambda s: (layer, 1, 0, 0)),
            pl.BlockSpec((None, None, 1, d), lambda s: (layer, 1, 0, 0)),
        ],
        out_specs=[
            pl.BlockSpec((tm, d), lambda s: (jnp.maximum(s // n_c - 1, 0), 0)),
            pl.BlockSpec((None, SUBLANES, tn), lambda s: (row(s), 0, chunk(s))),
        ],
        out_shape=[jax.ShapeDtypeStruct((m, d), F32),
                   jax.ShapeDtypeStruct((n_i, SUBLANES, dc), F32)],
        scratch_shapes=[pltpu.VMEM((tm, d), BF16), pltpu.VMEM((tm, d), F32), pltpu.VMEM((tm, d), F32),
                        pltpu.VMEM((n_c, SUBLANES, tn), F32)],
        compiler_params=_params("arbitrary"),
        name="conv_mixer_seq",
    )(x, prev, wb, wc, wh, w_conv, w_out, g, b)


def _conv_mixer_rows(x, p1, p2, f1, f2, w_in, w_conv, w_out, g, b, j, layer, alpha, tn_pref=256):
    m, d = x.shape
    dc = w_out.shape[1]
    tn = _tile(dc, tn_pref)
    tn_main = _tile(dc, CONV_TN)
    assert tn_main % tn == 0
    per = tn_main // tn
    n_c = dc // tn
    w_spec = lambda part: pl.BlockSpec((None, d, tn), lambda c: (j, 0, part * n_c + c))
    wb_out = pl.BlockSpec((None, d, tn), lambda c: (c // per, 0, c % per))
    wide_shape = jax.ShapeDtypeStruct((dc // tn_main, d, tn_main), BF16)
    return pl.pallas_call(
        functools.partial(_conv_rows_kernel, alpha=alpha),
        grid=(n_c,),
        in_specs=[
            pl.BlockSpec((m, d), lambda c: (0, 0)),
            pl.BlockSpec((m, tn), lambda c: (0, c)),
            pl.BlockSpec((m, tn), lambda c: (0, c)),
            pl.BlockSpec((m, 1), lambda c: (0, 0)),
            pl.BlockSpec((m, 1), lambda c: (0, 0)),
            w_spec(0), w_spec(1), w_spec(2),
            pl.BlockSpec((None, 3, tn), lambda c: (j, 0, c)),
            pl.BlockSpec((None, tn, d), lambda c: (j, c, 0)),
            pl.BlockSpec((None, None, 1, d), lambda c: (layer, 1, 0, 0)),
            pl.BlockSpec((None, None, 1, d), lambda c: (layer, 1, 0, 0)),
        ],
        out_specs=[
            pl.BlockSpec((m, d), lambda c: (0, 0)),
            pl.BlockSpec((m, tn), lambda c: (0, c)),
            wb_out, wb_out, wb_out,
            pl.BlockSpec((tn, d), lambda c: (c, 0)),
        ],
        out_shape=[jax.ShapeDtypeStruct((m, d), F32), jax.ShapeDtypeStruct((m, dc), F32),
                   wide_shape, wide_shape, wide_shape, jax.ShapeDtypeStruct((dc, d), BF16)],
        scratch_shapes=[pltpu.VMEM((m, d), BF16), pltpu.VMEM((m, d), F32)],
        compiler_params=_params("arbitrary"),
        name="conv_mixer_rows",
    )(x, p1, p2, f1, f2, w_in, w_in, w_in, w_conv, w_out, g, b)


def _mla_proj_kernel(x_ref, cos_ref, sin_ref, wqa_ref, gqa_ref, wqn_ref, wqr_ref, wqrs_ref, wkv_ref, gkv_ref,
                     wuk_ref, wuv_ref, qc_ref, kc_ref, v_ref, ckv_ref, kr_ref, *, n_heads, kv_lora, rope_dim):
    half = LANES // 2
    assert rope_dim == half, "rope halves are packed two heads per 128 lanes"
    xb = x_ref[...].astype(BF16)
    cos, sin = cos_ref[...], sin_ref[...]
    lane = lax.broadcasted_iota(jnp.int32, cos.shape, 1)
    low = lane < half

    cq = _rms_norm(_dot(xb, wqa_ref[...]), gqa_ref[...]).astype(BF16)
    qn = _dot(cq, wqn_ref[...])
    qr_a = _dot(cq, wqr_ref[...])
    qr_b = _dot(cq, wqrs_ref[...])
    for pair in range(n_heads // 2):
        sl = slice(pair * LANES, (pair + 1) * LANES)
        both = qr_a[:, sl] * cos + qr_b[:, sl] * sin
        for k in range(2):
            h = 2 * pair + k
            part = both if k == 0 else pltpu.roll(both, half, 1)
            qc_ref[:, h * 2 * LANES:h * 2 * LANES + LANES] = qn[:, h * LANES:(h + 1) * LANES].astype(BF16)
            qc_ref[:, h * 2 * LANES + LANES:(h + 1) * 2 * LANES] = jnp.where(low, part, 0.0).astype(BF16)

    kv = _dot(xb, wkv_ref[...])
    ckv = _rms_norm(kv[:, :kv_lora], gkv_ref[...])
    ckv_ref[...] = ckv
    kr = kv[:, kv_lora:kv_lora + LANES] * cos + kv[:, kv_lora + LANES:] * sin
    kr = jnp.where(low, kr, 0.0)
    kr_ref[...] = kr[:, :rope_dim]
    ckv_b = ckv.astype(BF16)
    kn = _dot(ckv_b, wuk_ref[...])
    v = _dot(ckv_b, wuv_ref[...])
    kr_b = kr.astype(BF16)
    ones = jnp.ones((xb.shape[0], LANES), BF16)
    for h in range(n_heads):
        kc_ref[:, h * 2 * LANES:h * 2 * LANES + LANES] = kn[:, h * LANES:(h + 1) * LANES].astype(BF16)
        kc_ref[:, h * 2 * LANES + LANES:(h + 1) * 2 * LANES] = kr_b
        v_ref[:, h * 2 * LANES:h * 2 * LANES + LANES] = v[:, h * LANES:(h + 1) * LANES].astype(BF16)
        v_ref[:, h * 2 * LANES + LANES:(h + 1) * 2 * LANES] = ones


def _mla_proj(x, cos, sin, wqa, gqa, wqn, wqr, wqrs, wkv, gkv, wuk, wuv, j, n_heads, tm_pref=256):
    m, d = x.shape
    kv_lora = wuk.shape[1]
    rope_dim = wqr.shape[-1] // n_heads
    tm = _tile(m, tm_pref)
    hq = n_heads * 2 * LANES
    hv = hq
    row = lambda w: pl.BlockSpec((tm, w), lambda i: (i, 0))
    full = lambda a: _resident((None,) + a.shape[1:], lambda i: (j,) + (0,) * (a.ndim - 1))
    return pl.pallas_call(
        functools.partial(_mla_proj_kernel, n_heads=n_heads, kv_lora=kv_lora, rope_dim=rope_dim),
        grid=(m // tm,),
        in_specs=[row(d), row(LANES), row(LANES), full(wqa), full(gqa), full(wqn), full(wqr), full(wqrs),
                  full(wkv), full(gkv), full(wuk), full(wuv)],
        out_specs=[row(hq), row(hq), row(hv), row(kv_lora), row(rope_dim)],
        out_shape=[jax.ShapeDtypeStruct((m, hq), BF16), jax.ShapeDtypeStruct((m, hq), BF16),
                   jax.ShapeDtypeStruct((m, hv), BF16), jax.ShapeDtypeStruct((m, kv_lora), F32),
                   jax.ShapeDtypeStruct((m, rope_dim), F32)],
        compiler_params=_params("parallel"),
        name="mla_proj",
    )(x, cos, sin, wqa, gqa, wqn, wqr, wqrs, wkv, gkv, wuk, wuv)


def _lane_chunks(s):
    return [s[:, c * LANES:(c + 1) * LANES] for c in range(s.shape[1] // LANES)]


def _row_max(parts):
    m = functools.reduce(jnp.maximum, parts)
    return jnp.broadcast_to(jnp.max(m, axis=1, keepdims=True), m.shape)


def _attn_kernel(q_ref, k_ref, v_ref, kp_ref, vp_ref, o_ref, m_ref, acc_ref, *, c_exp, tk, n_pre):
    qi = pl.program_id(2)
    dq = 2 * LANES
    n_grp = q_ref.shape[1] // dq
    assert q_ref.shape[0] == tk and v_ref.shape[1] == n_grp * dq
    qs = [q_ref[:, h * dq:(h + 1) * dq] for h in range(n_grp)]
    hq = lambda h: slice(h * dq, (h + 1) * dq)
    k_blk = lambda h, start: k_ref[pl.ds(start, tk), hq(h)]
    v_blk = lambda h, start: v_ref[pl.ds(start, tk), hq(h)]
    hl = lambda h: slice(h * LANES, (h + 1) * LANES)

    start = pl.multiple_of(qi * tk, tk)
    row = lax.broadcasted_iota(jnp.int32, (tk, tk), 0)
    col = lax.broadcasted_iota(jnp.int32, (tk, tk), 1)
    causal = col <= row
    real_pre = lax.broadcasted_iota(jnp.int32, (tk, LANES), 1) < n_pre
    for h in range(n_grp):
        s_d = jnp.where(causal, _dot_nt(qs[h], k_blk(h, start)), -jnp.inf)
        s_p = jnp.where(real_pre, _dot_nt(qs[h], kp_ref[:, hq(h)]), -jnp.inf)
        parts = _lane_chunks(s_d) + [s_p]
        m = _row_max(parts)
        p_parts = [jnp.exp2((x - m) * c_exp) for x in parts]
        m_ref[:, hl(h)] = m
        acc_ref[:, hq(h)] = (_dot(jnp.concatenate(p_parts[:-1], axis=1).astype(BF16), v_blk(h, start))
                             + _dot(p_parts[-1].astype(BF16), vp_ref[:, hq(h)]))

    def block(jb):
        start = pl.multiple_of(jb * tk, tk)
        for h in range(n_grp):
            parts = _lane_chunks(_dot_nt(qs[h], k_blk(h, start)))
            m_old = m_ref[:, hl(h)]
            m_new = jnp.maximum(m_old, _row_max(parts))
            a = jnp.exp2((m_old - m_new) * c_exp)
            p = jnp.concatenate([jnp.exp2((x - m_new) * c_exp) for x in parts], axis=1).astype(BF16)
            acc_ref[:, hq(h)] = jnp.concatenate([a, a], axis=1) * acc_ref[:, hq(h)] + _dot(p, v_blk(h, start))
            m_ref[:, hl(h)] = m_new

    def pair(jp, carry):
        block(2 * jp)
        block(2 * jp + 1)
        return carry

    lax.fori_loop(0, qi // 2, pair, 0)

    @pl.when(qi % 2 == 1)
    def _():
        block(qi - 1)

    for h in range(n_grp):
        acc = acc_ref[:, hq(h)]
        o_ref[:, hl(h)] = (acc[:, :LANES] / acc[:, LANES:]).astype(o_ref.dtype)


def _attention(q, k, v, k_pre, v_pre, n_pre, n_seq, seq_len, n_heads, scale, t_pref=512, heads_per_step=4):
    dq = 2 * LANES
    t = _tile(seq_len, t_pref)
    n_t = seq_len // t
    g = max(x for x in range(1, heads_per_step + 1) if n_heads % x == 0)
    assert t % LANES == 0 and k_pre.shape[0] == LANES and v.shape[1] == n_heads * dq
    return pl.pallas_call(
        functools.partial(_attn_kernel, c_exp=scale * LOG2_E, tk=t, n_pre=n_pre),
        grid=(n_seq, n_heads // g, n_t),
        in_specs=[
            pl.BlockSpec((t, g * dq), lambda b, h, i: (b * n_t + i, h)),
            pl.BlockSpec((seq_len, g * dq), lambda b, h, i: (b, h)),
            pl.BlockSpec((seq_len, g * dq), lambda b, h, i: (b, h)),
            pl.BlockSpec((LANES, g * dq), lambda b, h, i: (0, h)),
            pl.BlockSpec((LANES, g * dq), lambda b, h, i: (0, h)),
        ],
        out_specs=pl.BlockSpec((t, g * LANES), lambda b, h, i: (b * n_t + i, h)),
        out_shape=jax.ShapeDtypeStruct((n_seq * seq_len, n_heads * LANES), BF16),
        scratch_shapes=[pltpu.VMEM((t, g * LANES), F32), pltpu.VMEM((t, g * dq), F32)],
        compiler_params=_params("parallel", "parallel", "arbitrary"),
        name="attn_causal",
    )(q, k, v, k_pre, v_pre)


def _prefix_attn_kernel(q_ref, k_ref, v_ref, o_ref, *, scale):
    s = _dot_nt(q_ref[...], k_ref[...]) * scale
    row = lax.broadcasted_iota(jnp.int32, s.shape, 0)
    col = lax.broadcasted_iota(jnp.int32, s.shape, 1)
    s = jnp.where(col <= row, s, -jnp.inf)
    p = jnp.exp(s - jnp.max(s, axis=1, keepdims=True))
    o = _dot(p.astype(BF16), v_ref[:, :LANES]) / jnp.sum(p, axis=1, keepdims=True)
    o_ref[...] = o.astype(o_ref.dtype)


def _prefix_attention(q, k, v, n_heads, scale):
    n = q.shape[0]
    dq = 2 * LANES
    blk = pl.BlockSpec((n, dq), lambda h: (0, h))
    return pl.pallas_call(
        functools.partial(_prefix_attn_kernel, scale=scale),
        grid=(n_heads,),
        in_specs=[blk, blk, blk],
        out_specs=pl.BlockSpec((n, LANES), lambda h: (0, h)),
        out_shape=jax.ShapeDtypeStruct((n, n_heads * LANES), BF16),
        compiler_params=_params("parallel"),
        name="attn_prefix",
    )(q, k, v)


def _absorb_q_kernel(q_ref, wuk_ref, o_ref):
    o_ref[...] = _dot_nt(q_ref[:, :LANES], wuk_ref[...]).astype(o_ref.dtype)


def _absorb_q(q_cat, wuk, j, n_rows, n_heads):
    kv_lora = wuk.shape[1]
    return pl.pallas_call(
        _absorb_q_kernel,
        grid=(n_heads,),
        in_specs=[pl.BlockSpec((n_rows, 2 * LANES), lambda h: (0, h)),
                  pl.BlockSpec((None, kv_lora, LANES), lambda h: (j, 0, h))],
        out_specs=pl.BlockSpec((None, n_rows, kv_lora), lambda h: (h, 0, 0)),
        out_shape=jax.ShapeDtypeStruct((n_heads, n_rows, kv_lora), BF16),
        compiler_params=_params("parallel"),
        name="absorb_q",
    )(q_cat, wuk)


def _expand_o_kernel(o_ref, wuv_ref, out_ref):
    out_ref[...] = _dot(o_ref[...], wuv_ref[...]).astype(out_ref.dtype)


def _expand_o(o_lat, wuv, j, n_heads):
    _, n_rows, kv_lora = o_lat.shape
    hv = wuv.shape[-1] // n_heads
    return pl.pallas_call(
        _expand_o_kernel,
        grid=(n_heads,),
        in_specs=[pl.BlockSpec((None, n_rows, kv_lora), lambda h: (h, 0, 0)),
                  pl.BlockSpec((None, kv_lora, hv), lambda h: (j, 0, h))],
        out_specs=pl.BlockSpec((n_rows, hv), lambda h: (0, h)),
        out_shape=jax.ShapeDtypeStruct((n_rows, n_heads * hv), BF16),
        compiler_params=_params("parallel"),
        name="expand_o",
    )(o_lat, wuv)


def _sample_attn_kernel(pt_ref, ql_ref, qr_ref, cnew_ref, krnew_ref, ckv_hbm, krt_hbm, o_ref,
                        ckv_buf, krt_buf, sem, *, scale, layer, n_pages):
    b = pl.program_id(0)
    slot = b % 2

    def page_copies(seq, slot, p):
        page_id = pt_ref[seq * n_pages + p]
        return (pltpu.make_async_copy(ckv_hbm.at[layer, page_id], ckv_buf.at[slot, p], sem.at[slot, 0]),
                pltpu.make_async_copy(krt_hbm.at[layer, page_id], krt_buf.at[slot, p], sem.at[slot, 1]))

    def start_gather(seq, slot):
        for p in range(n_pages):
            for cp in page_copies(seq, slot, p):
                cp.start()

    @pl.when(b == 0)
    def _():
        start_gather(0, 0)

    @pl.when(b + 1 < pl.num_programs(0))
    def _():
        start_gather(b + 1, 1 - slot)

    for p in range(n_pages):
        for cp in page_copies(b, slot, p):
            cp.wait()

    ql, qr = ql_ref[...], qr_ref[...]
    cn = cnew_ref[...].astype(BF16).astype(F32)
    kn = krnew_ref[...].astype(BF16).astype(F32)
    s_new = (jnp.sum(ql.astype(F32) * cn, axis=-1, keepdims=True)
             + jnp.sum(qr.astype(F32) * kn, axis=-1, keepdims=True)) * scale

    pages = [ckv_buf[slot, p].astype(BF16) for p in range(n_pages)]
    s = jnp.concatenate(
        [_dot_nt(ql, pages[p]) + _dot(qr, krt_buf[slot, p].astype(BF16)) for p in range(n_pages)],
        axis=1) * scale
    m = jnp.maximum(jnp.max(s, axis=-1, keepdims=True), s_new)
    e = jnp.exp(s - m)
    e_new = jnp.exp(s_new - m)
    eb = e.astype(BF16)
    page = pages[0].shape[0]
    acc = e_new.astype(BF16).astype(F32) * cn
    for p in range(n_pages):
        acc += _dot(eb[:, p * page:(p + 1) * page], pages[p])
    o_ref[...] = (acc / (jnp.sum(e, axis=-1, keepdims=True) + e_new)).astype(o_ref.dtype)


def _sample_attention(q_lat, q_rope, c_new, kr_new, cache_ckv, cache_krt, page_table, j, scale):
    n, n_heads, kv_lora = q_lat.shape
    rope_dim = q_rope.shape[-1]
    n_pages, page = page_table.shape[1], cache_ckv.shape[2]
    per_row = lambda s1, w: pl.BlockSpec((None, s1, w), lambda b, pt: (b, 0, 0))
    hbm = pl.BlockSpec(memory_space=pl.ANY)
    grid_spec = pltpu.PrefetchScalarGridSpec(
        num_scalar_prefetch=1,
        grid=(n,),
        in_specs=[per_row(n_heads, kv_lora), per_row(n_heads, rope_dim), per_row(1, kv_lora), per_row(1, rope_dim),
                  hbm, hbm],
        out_specs=per_row(n_heads, kv_lora),
        scratch_shapes=[pltpu.VMEM((2, n_pages, page, kv_lora), cache_ckv.dtype),
                        pltpu.VMEM((2, n_pages, rope_dim, page), cache_krt.dtype),
                        pltpu.SemaphoreType.DMA((2, 2))],
    )
    return pl.pallas_call(
        functools.partial(_sample_attn_kernel, scale=scale, layer=j, n_pages=n_pages),
        grid_spec=grid_spec,
        out_shape=jax.ShapeDtypeStruct((n, n_heads, kv_lora), BF16),
        compiler_params=_params("arbitrary"),
        name="sample_attn",
    )(page_table.reshape(-1), q_lat, q_rope, c_new, kr_new, cache_ckv, cache_krt)


def _proj_ln_kernel(a_ref, x_ref, w_ref, g_ref, b_ref, o_ref, y_ref, *, alpha, n_i):
    s = pl.program_id(0)

    @pl.when(s == 0)
    def _():
        y_ref[...] = jnp.zeros_like(y_ref)

    @pl.when(s < n_i)
    def _():
        o_ref[...] = _layer_norm(y_ref[...], g_ref[...], b_ref[...])
        y_ref[...] = alpha * x_ref[...] + _dot(a_ref[...], w_ref[...])

    @pl.when(s == n_i)
    def _():
        o_ref[...] = _layer_norm(y_ref[...], g_ref[...], b_ref[...])


def _proj_ln(a, x, w, g, b, j, layer, alpha, tm_pref=512):
    m, d = x.shape
    k = a.shape[1]
    tm = _tile(m, tm_pref)
    n_i = m // tm
    row = lambda s: jnp.minimum(s, n_i - 1)
    return pl.pallas_call(
        functools.partial(_proj_ln_kernel, alpha=alpha, n_i=n_i),
        grid=(n_i + 1,),
        in_specs=[
            pl.BlockSpec((tm, k), lambda s: (row(s), 0)),
            pl.BlockSpec((tm, d), lambda s: (row(s), 0)),
            _resident((None, k, d), lambda s: (j, 0, 0)),
            pl.BlockSpec((None, None, 1, d), lambda s: (layer, 1, 0, 0)),
            pl.BlockSpec((None, None, 1, d), lambda s: (layer, 1, 0, 0)),
        ],
        out_specs=pl.BlockSpec((tm, d), lambda s: (jnp.maximum(s - 1, 0), 0)),
        out_shape=jax.ShapeDtypeStruct((m, d), F32),
        scratch_shapes=[pltpu.VMEM((tm, d), F32)],
        compiler_params=_params("arbitrary"),
        name="proj_ln",
    )(a, x, w, g, b)


def _rope_tables(pos, rope_dim):
    inv = ROPE_THETA ** (-jnp.arange(0, rope_dim, 2, dtype=F32) / rope_dim)
    ang = pos.astype(F32)[:, None] * inv[None, :]
    cos, sin = jnp.cos(ang), jnp.sin(ang)
    reps = LANES // rope_dim
    return jnp.tile(jnp.concatenate([cos, cos], -1), (1, reps)), jnp.tile(jnp.concatenate([-sin, sin], -1), (1, reps))


def _swap_halves(w):
    half = w.shape[-1] // 2
    return jnp.concatenate([w[..., half:], w[..., :half]], axis=-1)


def _pad_rows(a, rows):
    return jnp.pad(a, ((0, rows - a.shape[0]), (0, 0)))


def kernel(x_prompt, x_sample, state_conv, cache_kv_latent, cache_k_rope, page_table, meta_tokens, ln_g, ln_b, w_ffn_gate, w_ffn_up, w_ffn_down, w_conv_in, w_conv, w_conv_out, w_q_a, g_q_a, w_q_b, w_kv_a, g_kv_a, w_uk, w_uv, w_o):
    nb, seq, d = x_prompt.shape
    ns, dec_seq, _ = x_sample.shape
    assert dec_seq == 1, "the sample path handles one new token per sequence"
    n_meta = meta_tokens.shape[0]
    depth = ln_g.shape[0]
    alpha = float((2 * depth) ** 0.25)
    n_heads = w_q_b.shape[2]
    rope_dim = cache_k_rope.shape[-1]
    nope_dim = w_q_b.shape[3] - rope_dim
    kv_lora = w_uk.shape[1]
    scale = float((nope_dim + rope_dim) ** -0.5)
    past = page_table.shape[1] * cache_kv_latent.shape[2]
    assert nope_dim == LANES and w_uv.shape[-1] == LANES and n_heads % 2 == 0 and n_meta <= LANES

    xm = x_prompt.reshape(nb * seq, d)
    xe = jnp.concatenate([x_sample.reshape(ns, d), meta_tokens.astype(x_prompt.dtype)], axis=0)
    n_ext = ns + n_meta

    wqa = w_q_a.astype(BF16)
    wqn = w_q_b[..., :nope_dim].reshape(w_q_b.shape[0], w_q_b.shape[1], n_heads * nope_dim).astype(BF16)
    wq_rope = w_q_b[..., nope_dim:]
    wqr = wq_rope.reshape(w_q_b.shape[0], w_q_b.shape[1], n_heads * rope_dim).astype(BF16)
    wqrs = _swap_halves(wq_rope).reshape(wqr.shape).astype(BF16)
    wk_rope = w_kv_a[..., kv_lora:]
    zpad = jnp.zeros(wk_rope.shape[:-1] + (LANES - rope_dim,), w_kv_a.dtype)
    wkv = jnp.concatenate([w_kv_a[..., :kv_lora], wk_rope, zpad, _swap_halves(wk_rope), zpad], axis=-1).astype(BF16)
    wuk = w_uk.reshape(w_uk.shape[0], kv_lora, n_heads * nope_dim).astype(BF16)
    wuv = w_uv.reshape(w_uv.shape[0], kv_lora, -1).astype(BF16)
    wo = w_o.astype(BF16)
    ln_g4, ln_b4 = ln_g[:, :, None, :], ln_b[:, :, None, :]
    g_q3, g_kv3 = g_q_a[:, None, :], g_kv_a[:, None, :]
    cache_krt = jnp.swapaxes(cache_k_rope, 2, 3)

    cos_m, sin_m = _rope_tables(jnp.tile(n_meta + jnp.arange(seq), nb), rope_dim)
    cos_e, sin_e = _rope_tables(jnp.concatenate([jnp.full((ns,), past), jnp.arange(n_meta)]), rope_dim)

    ffn_w4 = (w_ffn_gate, w_ffn_up, w_ffn_down)
    piggyback = _can_cast_next(nb * seq, d, w_ffn_gate.shape[-1])
    ready = {}

    def ffn_both(xm, xe, layer, half, ln_idx):
        nxt = (layer, 1) if half == 0 else (layer + 1, 0)
        cast_next = ffn_w4 + nxt if piggyback and nxt[0] < depth else None
        if (layer, half) in ready:
            w = ready.pop((layer, half))
            xm, xe, *w_next = _ffn(xm, *w, ln_g4, ln_b4, layer, ln_idx, alpha, ext=xe, cast_next=cast_next)
        else:
            xe, *w = _ffn_cast(xe, *ffn_w4, ln_g4, ln_b4, layer, half, ln_idx, alpha)
            xm, *w_next = _ffn(xm, *w, ln_g4, ln_b4, layer, ln_idx, alpha, cast_next=cast_next)
        if cast_next is not None:
            ready[nxt] = w_next
        return xm, xe

    conv_p, conv_s, ckv_p, ckv_s, kr_p, kr_s = [], [], [], [], [], []
    for i in range(depth):
        j = i // N_MIXERS
        xm, xe = ffn_both(xm, xe, i, 0, 0)
        if i % N_MIXERS == 0:
            dc = w_conv_out.shape[1]
            st = state_conv[j]
            zeros = jnp.zeros((n_meta, dc), F32)
            p1 = jnp.concatenate([st[:, 1], zeros], axis=0)
            p2 = jnp.concatenate([st[:, 0], zeros], axis=0)
            rows = jnp.arange(n_ext)[:, None]
            f1 = (rows <= ns).astype(F32)
            f2 = (rows <= ns + 1).astype(F32)
            xe, u_e, wb, wc, wh, wout = _conv_mixer_rows(xe, p1, p2, f1, f2, w_conv_in, w_conv, w_conv_out,
                                                         ln_g4, ln_b4, j, i, alpha)
            prev = jnp.broadcast_to(jnp.pad(u_e[n_ext - 2:], ((SUBLANES - 2, 0), (0, 0)))[None], (nb, SUBLANES, dc))
            xm, u_last = _conv_mixer_seq(xm, prev, wb, wc, wh, w_conv, wout, ln_g4, ln_b4, j, i, alpha, seq)
            tiles_per_seq = u_last.shape[0] // nb
            conv_p.append(u_last[tiles_per_seq - 1::tiles_per_seq, SUBLANES - 2:])
            conv_s.append(jnp.stack([st[:, 1], u_e[:ns]], axis=1))
        else:
            proj = functools.partial(_mla_proj, wqa=wqa, gqa=g_q3, wqn=wqn, wqr=wqr, wqrs=wqrs, wkv=wkv, gkv=g_kv3,
                                     wuk=wuk, wuv=wuv, j=j, n_heads=n_heads)
            qc_e, kc_e, v_e, ckv_e, kr_e = proj(xe, cos_e, sin_e)
            qc_m, kc_m, v_m, ckv_m, kr_m = proj(xm, cos_m, sin_m)
            o_meta = _prefix_attention(qc_e[ns:], kc_e[ns:], v_e[ns:], n_heads, scale)
            o_main = _attention(qc_m, kc_m, v_m, _pad_rows(kc_e[ns:], LANES), _pad_rows(v_e[ns:], LANES), n_meta,
                                nb, seq, n_heads, scale)
            q_lat = _absorb_q(qc_e, wuk, j, ns, n_heads).transpose(1, 0, 2)
            q_rope_s = qc_e[:ns].reshape(ns, n_heads, 2 * LANES)[:, :, LANES:LANES + rope_dim]
            o_lat = _sample_attention(q_lat, q_rope_s, ckv_e[:ns, None], kr_e[:ns, None],
                                      cache_kv_latent, cache_krt, page_table, j, scale)
            o_s = _expand_o(o_lat.transpose(1, 0, 2), wuv, j, n_heads)
            o_e = jnp.concatenate([o_s, o_meta], axis=0)
            xe = _proj_ln(o_e, xe, wo, ln_g4, ln_b4, j, i, alpha)
            xm = _proj_ln(o_main, xm, wo, ln_g4, ln_b4, j, i, alpha)
            meta_rows = lambda a: jnp.broadcast_to(a[ns:][None], (nb, n_meta, a.shape[-1]))
            ckv_p.append(jnp.concatenate([meta_rows(ckv_e), ckv_m.reshape(nb, seq, kv_lora)], axis=1))
            kr_p.append(jnp.concatenate([meta_rows(kr_e), kr_m.reshape(nb, seq, rope_dim)], axis=1))
            ckv_s.append(ckv_e[:ns, None])
            kr_s.append(kr_e[:ns, None])
        xm, xe = ffn_both(xm, xe, i, 1, 2)

    y_prompt = xm.reshape(nb, seq, d)
    y_sample = xe[:ns].reshape(ns, dec_seq, d)
    return (y_prompt, y_sample, jnp.stack(conv_p), jnp.stack(conv_s), jnp.stack(ckv_p), jnp.stack(ckv_s),
            jnp.stack(kr_p), jnp.stack(kr_s))
```
